```python
import math
import jax, jax.numpy as jnp
from jax import lax
import numpy as np

D_MODEL = 1024
BATCH = 2
SEQ = 8192
DEPTH = 2

CHUNK = 64
Q_BLOCK = 128
DA_HEADS = 8
DA_HEAD_DIM = 64
SB_HEADS = 16
SB_HEAD_DIM = 64
DA_QK = DA_HEADS * DA_HEAD_DIM
DA_V = DA_HEADS * 2 * DA_HEAD_DIM
SB_W = SB_HEADS * SB_HEAD_DIM
IN_SIZES = (DA_QK, DA_QK, DA_QK, DA_QK, DA_V, SB_W, SB_W, SB_W, D_MODEL, D_MODEL)
IN_COLS = sum(IN_SIZES)
N_EXPERTS = 32
TOP_K = 4
D_EXPERT = D_MODEL
SWIGLU_LIMIT = 7.0
SWIGLU_ALPHA = 1.702
EXPERT_BLOCK = 256
EPS = 1e-6

kernel_name = "hybrid_diffattn_stickbreak_moe_adaln"


def rms_norm(x, g):
    x32 = x.astype(jnp.float32)
    y = x32 * lax.rsqrt(jnp.mean(x32 * x32, axis=-1, keepdims=True) + EPS)
    return (y * g.astype(jnp.float32)).astype(x.dtype)


def alibi_slopes(n_heads):
    return jnp.asarray([2.0 ** (-8.0 * (i + 1) / n_heads) for i in range(n_heads)], jnp.float32)


def to_qblocks(a):
    b, s, h, d = a.shape
    return a.reshape(b, s // Q_BLOCK, Q_BLOCK, h, d).transpose(1, 0, 3, 2, 4)


def from_qblocks(a):
    nb, b, h, qb, d = a.shape
    return a.transpose(1, 0, 3, 2, 4).reshape(b, nb * qb, h, d)


def differential_attention(q1, q2, k1, k2, v, lam):
    s_len = q1.shape[1]
    scale = 1.0 / math.sqrt(DA_HEAD_DIM)
    k1t = k1.astype(jnp.float32).transpose(0, 2, 1, 3)
    k2t = k2.astype(jnp.float32).transpose(0, 2, 1, 3)
    vt = v.astype(jnp.float32).transpose(0, 2, 1, 3)
    slopes = alibi_slopes(DA_HEADS)
    kpos = jnp.arange(s_len)

    def step(args):
        qa, qb, bi = args
        qpos = bi * Q_BLOCK + jnp.arange(Q_BLOCK)
        allowed = (kpos[None, :] // CHUNK) <= (qpos[:, None] // CHUNK)
        dist = jnp.abs(qpos[:, None] - kpos[None, :]).astype(jnp.float32)
        bias = -slopes[:, None, None] * dist[None]
        s1 = jnp.einsum('bhqd,bhkd->bhqk', qa, k1t) * scale + bias
        s2 = jnp.einsum('bhqd,bhkd->bhqk', qb, k2t) * scale + bias
        s1 = jnp.where(allowed, s1, -jnp.inf)
        s2 = jnp.where(allowed, s2, -jnp.inf)
        w = jax.nn.softmax(s1, axis=-1) - lam * jax.nn.softmax(s2, axis=-1)
        return jnp.einsum('bhqk,bhkd->bhqd', w, vt)

    nb = s_len // Q_BLOCK
    out = lax.map(step, (to_qblocks(q1.astype(jnp.float32)), to_qblocks(q2.astype(jnp.float32)),
                         jnp.arange(nb)))
    return from_qblocks(out)


def stick_breaking_attention(q, k, v):
    s_len = q.shape[1]
    scale = 1.0 / math.sqrt(SB_HEAD_DIM)
    kt = k.astype(jnp.float32).transpose(0, 2, 1, 3)
    vt = v.astype(jnp.float32).transpose(0, 2, 1, 3)
    kpos = jnp.arange(s_len)

    def step(args):
        qb, bi = args
        qpos = bi * Q_BLOCK + jnp.arange(Q_BLOCK)
        strict = kpos[None, :] < qpos[:, None]
        z = jnp.einsum('bhqd,bhkd->bhqk', qb, kt) * scale
        log1m = jnp.where(strict, -jax.nn.softplus(z), 0.0)
        suffix = lax.cumsum(log1m, axis=3, reverse=True) - log1m
        a = jnp.where(strict, jnp.exp(jax.nn.log_sigmoid(z) + suffix), 0.0)
        return jnp.einsum('bhqk,bhkd->bhqd', a, vt)

    nb = s_len // Q_BLOCK
    out = lax.map(step, (to_qblocks(q.astype(jnp.float32)), jnp.arange(nb)))
    return from_qblocks(out)


def token_mixer(h, layer, w_in, lam_q1, lam_k1, lam_q2, lam_k2, subln_g, w_pa, w_pb, w_o):
    b, s, _ = h.shape
    proj = h @ w_in
    offs = np.cumsum(IN_SIZES)[:-1].tolist()
    q1, q2, k1, k2, va, qs, ks, vs, ga, gb = jnp.split(proj, offs, axis=-1)
    hd = (b, s, DA_HEADS, DA_HEAD_DIM)
    lam_init = 0.8 - 0.6 * math.exp(-0.3 * layer)
    lam = (jnp.exp(jnp.sum(lam_q1.astype(jnp.float32) * lam_k1.astype(jnp.float32)))
           - jnp.exp(jnp.sum(lam_q2.astype(jnp.float32) * lam_k2.astype(jnp.float32))) + lam_init)
    oa = differential_attention(q1.reshape(hd), q2.reshape(hd), k1.reshape(hd), k2.reshape(hd),
                                va.reshape(b, s, DA_HEADS, 2 * DA_HEAD_DIM), lam)
    oa = rms_norm(oa, subln_g) * (1.0 - lam_init)
    oa = oa.reshape(b, s, DA_V).astype(h.dtype)
    sd = (b, s, SB_HEADS, SB_HEAD_DIM)
    ob = stick_breaking_attention(qs.reshape(sd), ks.reshape(sd), vs.reshape(sd))
    ob = ob.reshape(b, s, SB_W).astype(h.dtype)
    merged = jax.nn.sigmoid(ga) * (oa @ w_pa) + jax.nn.sigmoid(gb) * (ob @ w_pb)
    return merged @ w_o


def moe_ffn(h, router_w, router_b, w_gu, b_gu, w_down, b_down):
    b, s, d = h.shape
    n = b * s
    t = h.reshape(n, d)
    logits = (t @ router_w + router_b).astype(jnp.float32)
    top_val, top_idx = lax.top_k(logits, TOP_K)
    gates = jax.nn.softmax(top_val, axis=-1)
    m = n * TOP_K
    e = top_idx.reshape(m)
    tok = jnp.arange(m) // TOP_K
    wts = gates.reshape(m)
    order = jnp.argsort(e, stable=True)
    e_s = e[order]
    counts = jnp.bincount(e, length=N_EXPERTS)
    padded = ((counts + EXPERT_BLOCK - 1) // EXPERT_BLOCK) * EXPERT_BLOCK
    pad_end = jnp.cumsum(padded)
    pad_start = pad_end - padded
    grp_start = jnp.cumsum(counts) - counts
    dest = pad_start[e_s] + (jnp.arange(m) - grp_start[e_s])
    n_blocks = -(-m // EXPERT_BLOCK) + N_EXPERTS
    n_slots = n_blocks * EXPERT_BLOCK
    slot_tok = jnp.full((n_slots,), n, jnp.int32).at[dest].set(tok[order].astype(jnp.int32))
    slot_w = jnp.zeros((n_slots,), jnp.float32).at[dest].set(wts[order])
    blk_exp = jnp.minimum(jnp.searchsorted(pad_end, jnp.arange(n_blocks) * EXPERT_BLOCK, side='right'),
                          N_EXPERTS - 1)
    t_pad = jnp.concatenate([t, jnp.zeros((1, d), t.dtype)], axis=0)
    xb = t_pad[slot_tok].reshape(n_blocks, EXPERT_BLOCK, d)

    def expert_block(args):
        xe, eid = args
        gu = xe @ w_gu[eid] + b_gu[eid]
        g, u = jnp.split(gu, 2, axis=-1)
        g = jnp.minimum(g, SWIGLU_LIMIT)
        u = jnp.clip(u, -SWIGLU_LIMIT, SWIGLU_LIMIT)
        act = (u + 1.0) * (g * jax.nn.sigmoid(SWIGLU_ALPHA * g))
        return act @ w_down[eid] + b_down[eid]

    yb = lax.map(expert_block, (xb, blk_exp))
    y = (yb.reshape(n_slots, d).astype(jnp.float32) * slot_w[:, None]).astype(t.dtype)
    out = jax.ops.segment_sum(y, slot_tok, num_segments=n + 1)[:n]
    return out.reshape(b, s, d)


def setup_inputs(seed: int = 0) -> dict:
    key = jax.random.key(seed)
    ks = jax.random.split(key, 24)
    f32 = jnp.float32
    D, E, F = D_MODEL, N_EXPERTS, D_EXPERT
    nrm = lambda k, shape, s: jax.random.normal(k, shape, f32) * s
    return {
        "x": nrm(ks[0], (BATCH, SEQ, D), 1.0),
        "c": nrm(ks[1], (BATCH, D), 1.0),
        "w_ada": nrm(ks[2], (DEPTH, D, 6 * D), 0.5 * D ** -0.5),
        "b_ada": nrm(ks[3], (DEPTH, 6 * D), 0.02),
        "norm1_g": 1.0 + nrm(ks[4], (DEPTH, D), 0.02),
        "w_in": nrm(ks[5], (DEPTH, D, IN_COLS), D ** -0.5),
        "lam_q1": nrm(ks[6], (DEPTH, DA_HEAD_DIM), 0.1),
        "lam_k1": nrm(ks[7], (DEPTH, DA_HEAD_DIM), 0.1),
        "lam_q2": nrm(ks[8], (DEPTH, DA_HEAD_DIM), 0.1),
        "lam_k2": nrm(ks[9], (DEPTH, DA_HEAD_DIM), 0.1),
        "subln_g": 1.0 + nrm(ks[10], (DEPTH, 2 * DA_HEAD_DIM), 0.02),
        "w_pa": nrm(ks[11], (DEPTH, DA_V, D), DA_V ** -0.5),
        "w_pb": nrm(ks[12], (DEPTH, SB_W, D), SB_W ** -0.5),
        "w_o": nrm(ks[13], (DEPTH, D, D), D ** -0.5),
        "norm2_g": 1.0 + nrm(ks[14], (DEPTH, D), 0.02),
        "router_w": nrm(ks[15], (DEPTH, D, E), D ** -0.5),
        "router_b": nrm(ks[16], (DEPTH, E), 0.01),
        "w_gu": nrm(ks[17], (DEPTH, E, D, 2 * F), D ** -0.5),
        "b_gu": nrm(ks[18], (DEPTH, E, 2 * F), 0.02),
        "w_down": nrm(ks[19], (DEPTH, E, F, D), F ** -0.5),
        "b_down": nrm(ks[20], (DEPTH, E, D), 0.02),
        "final_g": 1.0 + nrm(ks[21], (D,), 0.02),
    }


def reference(x, c, w_ada, b_ada, norm1_g, w_in, lam_q1, lam_k1, lam_q2, lam_k2, subln_g,
              w_pa, w_pb, w_o, norm2_g, router_w, router_b, w_gu, b_gu, w_down, b_down, final_g):
    for l in range(DEPTH):
        mod = (c @ w_ada[l] + b_ada[l])[:, None, :]
        sh1, sc1, g1, sh2, sc2, g2 = jnp.split(mod, 6, axis=-1)
        h = rms_norm(x, norm1_g[l]) * (1.0 + sc1) + sh1
        x = x + g1 * token_mixer(h, l, w_in[l], lam_q1[l], lam_k1[l], lam_q2[l], lam_k2[l],
                                 subln_g[l], w_pa[l], w_pb[l], w_o[l])
        h = rms_norm(x, norm2_g[l]) * (1.0 + sc2) + sh2
        x = x + g2 * moe_ffn(h, router_w[l], router_b[l], w_gu[l], b_gu[l], w_down[l], b_down[l])
    return rms_norm(x, final_g)
```

```python
import functools
import math

import jax
import jax.numpy as jnp
from jax import lax
from jax.experimental import pallas as pl
from jax.experimental.pallas import tpu as pltpu

F32 = jnp.float32
BF16 = jnp.bfloat16

LANE = 128
VMEM_LIMIT = 56 * 1024 * 1024

CHUNK = 64
DA_HEADS = 8
DA_HEAD_DIM = 64
SB_HEADS = 16
SB_HEAD_DIM = 64
N_EXPERTS = 32
TOP_K = 4
SWIGLU_LIMIT = 7.0
SWIGLU_ALPHA = 1.702
EPS = 1e-6

NEG = -1e30
F32_EXP_ZERO = 104.0

EXPERT_BLOCK = 256


def _params(sem, vmem=VMEM_LIMIT):
    return pltpu.CompilerParams(dimension_semantics=sem, vmem_limit_bytes=vmem)


def _dot_nt(a, b):
    return lax.dot_general(a, b, (((1,), (1,)), ((), ())), preferred_element_type=F32)


def _dot(a, b):
    return jnp.dot(a, b, preferred_element_type=F32)


def _ada_kernel(c_ref, w_ref, b_ref, o_ref):
    o_ref[0] = jnp.dot(c_ref[...], w_ref[0], preferred_element_type=F32,
                       precision=lax.Precision.HIGHEST) + b_ref[0]


def _ada_mod(c8, w_ada, b_ada):
    depth, d, d6 = w_ada.shape
    tn = 2048
    return pl.pallas_call(
        _ada_kernel,
        grid=(depth, d6 // tn),
        in_specs=[pl.BlockSpec((8, d), lambda l, j: (0, 0)),
                  pl.BlockSpec((1, d, tn), lambda l, j: (l, 0, j)),
                  pl.BlockSpec((1, 1, tn), lambda l, j: (l, 0, j))],
        out_specs=pl.BlockSpec((1, 8, tn), lambda l, j: (l, 0, j)),
        out_shape=jax.ShapeDtypeStruct((depth, 8, d6), F32),
        compiler_params=_params(("parallel", "parallel")),
        name="ada_mod",
    )(c8, w_ada, b_ada.reshape(depth, 1, d6))


def _rms_mod(x, g, sc, sh):
    y = x * lax.rsqrt(jnp.mean(x * x, axis=-1, keepdims=True) + EPS)
    return (y * g) * (1.0 + sc) + sh


def _inproj_kernel(x_ref, mod_ref, g_ref, w_ref, o_ref, h_scr):
    @pl.when(pl.program_id(1) == 0)
    def _():
        h = _rms_mod(x_ref[...], g_ref[...], mod_ref[0, 1:2, :], mod_ref[0, 0:1, :])
        h_scr[...] = h.astype(BF16)

    o_ref[...] = _dot(h_scr[...], w_ref[...]).astype(BF16)


def _inproj(x2, modv, g, w_bf, seq):
    n, d = x2.shape
    cols = w_bf.shape[1]
    tm = min(1024, seq)
    tn = 1024
    tiles_per_batch = seq // tm
    return pl.pallas_call(
        _inproj_kernel,
        grid=(n // tm, cols // tn),
        in_specs=[pl.BlockSpec((tm, d), lambda i, j: (i, 0)),
                  pl.BlockSpec((1, 8, d), lambda i, j: (i // tiles_per_batch, 0, 0)),
                  pl.BlockSpec((1, d), lambda i, j: (0, 0)),
                  pl.BlockSpec((d, tn), lambda i, j: (0, j))],
        out_specs=pl.BlockSpec((tm, tn), lambda i, j: (i, j)),
        out_shape=jax.ShapeDtypeStruct((n, cols), BF16),
        scratch_shapes=[pltpu.VMEM((tm, d), BF16)],
        compiler_params=_params(("parallel", "arbitrary")),
        name="inproj",
    )(x2, modv, g, w_bf)


def _da_kernel(slopes_ref, q1_ref, q2_ref, k1_ref, k2_ref, v_ref, lamp_ref, subg_ref, o_ref,
               *, tq, lam_init):
    h = pl.program_id(1)
    qi = pl.program_id(2)
    slope = slopes_ref[h]
    lane = lax.broadcasted_iota(jnp.int32, (1, LANE), 1)
    mk = (lane // DA_HEAD_DIM) == (h % 2)
    scale = 1.0 / math.sqrt(DA_HEAD_DIM)
    zero = jnp.zeros((), BF16)
    q1 = jnp.where(mk, q1_ref[...], zero) * jnp.asarray(scale, BF16)
    q2 = jnp.where(mk, q2_ref[...], zero) * jnp.asarray(scale, BF16)
    q0 = qi * tq
    col = lax.broadcasted_iota(jnp.int32, (1, tq), 1)

    def update(s, m, l, acc, v):
        m_new = jnp.maximum(m, jnp.max(s, axis=-1, keepdims=True))
        alpha = jnp.exp(m - m_new)
        p = jnp.exp(s - m_new)
        l = alpha * l + jnp.sum(p, axis=-1, keepdims=True)
        acc = alpha * acc + _dot(p.astype(BF16), v)
        return m_new, l, acc

    def off_body(kb, carry):
        m1, l1, a1, m2, l2, a2 = carry
        k0 = pl.multiple_of(kb * tq, tq)
        v = v_ref[pl.ds(k0, tq), :]
        colb = slope * (k0 - q0 + col).astype(F32)
        s1 = _dot_nt(q1, k1_ref[pl.ds(k0, tq), :]) + colb
        m1, l1, a1 = update(s1, m1, l1, a1, v)
        s2 = _dot_nt(q2, k2_ref[pl.ds(k0, tq), :]) + colb
        m2, l2, a2 = update(s2, m2, l2, a2, v)
        return m1, l1, a1, m2, l2, a2

    m0 = jnp.full((tq, 1), NEG, F32)
    l0 = jnp.zeros((tq, 1), F32)
    a0 = jnp.zeros((tq, LANE), F32)
    carry = lax.fori_loop(0, qi, off_body, (m0, l0, a0, m0, l0, a0))
    m1, l1, a1, m2, l2, a2 = carry

    r = lax.broadcasted_iota(jnp.int32, (tq, tq), 0)
    c = lax.broadcasted_iota(jnp.int32, (tq, tq), 1)
    allowed = (c // CHUNK) <= (r // CHUNK)
    bias = slope * (r - jnp.abs(r - c)).astype(F32)
    k0 = pl.multiple_of(q0, tq)
    v = v_ref[pl.ds(k0, tq), :]
    s1 = jnp.where(allowed, _dot_nt(q1, k1_ref[pl.ds(k0, tq), :]) + bias, NEG)
    m1, l1, a1 = update(s1, m1, l1, a1, v)
    s2 = jnp.where(allowed, _dot_nt(q2, k2_ref[pl.ds(k0, tq), :]) + bias, NEG)
    m2, l2, a2 = update(s2, m2, l2, a2, v)

    lp = lamp_ref[...]
    lam = (jnp.exp(jnp.sum(lp[0:1] * lp[1:2], axis=-1, keepdims=True))
           - jnp.exp(jnp.sum(lp[2:3] * lp[3:4], axis=-1, keepdims=True)) + lam_init)
    o = a1 / l1 - lam * (a2 / l2)
    y = o * lax.rsqrt(jnp.mean(o * o, axis=-1, keepdims=True) + EPS)
    o_ref[...] = ((y * subg_ref[...]) * (1.0 - lam_init)).astype(BF16)


def _da_attention(proj, slopes, lamp, subg, batch, seq, lam_init):
    n = proj.shape[0]
    tq = min(512, seq)
    nq = seq // tq
    kern = functools.partial(_da_kernel, tq=tq, lam_init=lam_init)
    qspec = lambda off: pl.BlockSpec((tq, LANE), lambda b, h, i: (b * nq + i, off + h // 2))
    kspec = lambda off: pl.BlockSpec((seq, LANE), lambda b, h, i: (b, off + h // 2))
    return pl.pallas_call(
        kern,
        grid=(batch, DA_HEADS, nq),
        in_specs=[pl.BlockSpec(memory_space=pltpu.SMEM),
                  qspec(0), qspec(4), kspec(8), kspec(12),
                  pl.BlockSpec((seq, LANE), lambda b, h, i: (b, 16 + h)),
                  pl.BlockSpec((4, DA_HEAD_DIM), lambda b, h, i: (0, 0)),
                  pl.BlockSpec((1, LANE), lambda b, h, i: (0, 0))],
        out_specs=pl.BlockSpec((tq, LANE), lambda b, h, i: (b * nq + i, h)),
        out_shape=jax.ShapeDtypeStruct((n, DA_HEADS * 2 * DA_HEAD_DIM), BF16),
        compiler_params=_params(("parallel", "parallel", "arbitrary")),
        name="da_attn",
    )(slopes, proj, proj, proj, proj, proj, lamp, subg)


def _sb_kernel(q_ref, k_ref, v_ref, o_ref, *, tqb, ts):
    qi = pl.program_id(2)
    lane = lax.broadcasted_iota(jnp.int32, (1, LANE), 1)
    r = lax.broadcasted_iota(jnp.int32, (ts, ts), 0)
    c = lax.broadcasted_iota(jnp.int32, (ts, ts), 1)
    upper = (r > c).astype(BF16)
    strict = c < r
    scale = jnp.asarray(1.0 / math.sqrt(SB_HEAD_DIM), BF16)
    zero = jnp.zeros((), BF16)

    for sub in range(tqb // ts):
        kb_diag = qi * (tqb // ts) + sub
        acc_heads = []
        for hs in range(2):
            mk = (lane // SB_HEAD_DIM) == hs
            q = jnp.where(mk, q_ref[sub * ts:(sub + 1) * ts, :], zero) * scale

            def cond(carry):
                kb, done, _, _ = carry
                return jnp.logical_and(kb >= 0, done == 0)

            def body(carry, q=q, kb_diag=kb_diag):
                kb, _, rsum, acc = carry
                k0 = pl.multiple_of(kb * ts, ts)
                z = _dot_nt(q, k_ref[pl.ds(k0, ts), :])
                t = jnp.log1p(jnp.exp(-jnp.abs(z)))
                sp = jnp.maximum(z, 0.0) + t
                logsig = jnp.minimum(z, 0.0) - t
                valid = jnp.logical_or(strict, kb != kb_diag)
                sp = jnp.where(valid, sp, 0.0)
                hi = sp.astype(BF16)
                lo = (sp - hi.astype(F32)).astype(BF16)
                later = _dot(hi, upper) + _dot(lo, upper)
                a = jnp.where(valid, jnp.exp(logsig - (rsum + later)), 0.0)
                acc = acc + _dot(a.astype(BF16), v_ref[pl.ds(k0, ts), :])
                rsum = rsum + jnp.sum(sp, axis=-1, keepdims=True)
                done = (jnp.min(rsum) > F32_EXP_ZERO).astype(jnp.int32)
                return kb - 1, done, rsum, acc

            init = (kb_diag, jnp.int32(0), jnp.zeros((ts, 1), F32), jnp.zeros((ts, LANE), F32))
            _, _, _, acc = lax.while_loop(cond, body, init)
            acc_heads.append(acc)
        o_ref[sub * ts:(sub + 1) * ts, :] = jnp.where(
            (lane // SB_HEAD_DIM) == 0, acc_heads[0], acc_heads[1]).astype(BF16)


def _sb_attention(proj, batch, seq):
    n = proj.shape[0]
    tqb = min(512, seq)
    ts = 128
    nq = seq // tqb
    pairs = SB_HEADS // 2
    kern = functools.partial(_sb_kernel, tqb=tqb, ts=ts)
    return pl.pallas_call(
        kern,
        grid=(batch, pairs, nq),
        in_specs=[pl.BlockSpec((tqb, LANE), lambda b, p, i: (b * nq + i, 24 + p)),
                  pl.BlockSpec((seq, LANE), lambda b, p, i: (b, 32 + p)),
                  pl.BlockSpec((seq, LANE), lambda b, p, i: (b, 40 + p))],
        out_specs=pl.BlockSpec((tqb, LANE), lambda b, p, i: (b * nq + i, p)),
        out_shape=jax.ShapeDtypeStruct((n, SB_HEADS * SB_HEAD_DIM), BF16),
        compiler_params=_params(("parallel", "parallel", "arbitrary")),
        name="sb_attn",
    )(proj, proj, proj)


def _post_kernel(oa_ref, ob_ref, ga_ref, gb_ref, x_ref, wpa_ref, wpb_ref, wo_ref, mod_ref, g2_ref,
                 rw_ref, rb_ref, x_out, h_out, idx_out, gate_out, rank_out, cnt_out, cnt_scr, *, tm):
    @pl.when(pl.program_id(0) == 0)
    def _():
        cnt_scr[...] = jnp.zeros_like(cnt_scr)

    pa = _dot(oa_ref[...], wpa_ref[...])
    pb = _dot(ob_ref[...], wpb_ref[...])
    merged = (jax.nn.sigmoid(ga_ref[...].astype(F32)) * pa
              + jax.nn.sigmoid(gb_ref[...].astype(F32)) * pb)
    att = _dot(merged.astype(BF16), wo_ref[...])
    x = x_ref[...] + mod_ref[0, 2:3, :] * att
    x_out[...] = x
    h = _rms_mod(x, g2_ref[...], mod_ref[0, 4:5, :], mod_ref[0, 3:4, :])
    h_out[...] = h

    logits = jnp.dot(h, rw_ref[...], preferred_element_type=F32,
                     precision=lax.Precision.HIGHEST) + rb_ref[...]
    lane = lax.broadcasted_iota(jnp.int32, (1, LANE), 1)
    lane_f = lane.astype(F32)
    work = logits
    idx_acc = jnp.zeros((tm, LANE), F32)
    val_acc = jnp.zeros((tm, LANE), F32)
    onehot = jnp.zeros((tm, LANE), F32)
    sels = []
    top0 = None
    for k in range(TOP_K):
        mx = jnp.max(work, axis=-1, keepdims=True)
        sel = jnp.min(jnp.where(work == mx, lane_f, float(LANE)), axis=-1, keepdims=True)
        hit = lane_f == sel
        if k == 0:
            top0 = mx
        idx_acc = jnp.where(lane == k, sel, idx_acc)
        val_acc = jnp.where(lane == k, jnp.exp(mx - top0), val_acc)
        onehot = onehot + hit.astype(F32)
        work = jnp.where(hit, NEG * 2.0, work)
        sels.append(hit)
    gate_out[...] = val_acc / jnp.sum(val_acc, axis=-1, keepdims=True)
    idx_out[...] = idx_acc.astype(jnp.int32)

    rr = lax.broadcasted_iota(jnp.int32, (tm, tm), 0)
    cc = lax.broadcasted_iota(jnp.int32, (tm, tm), 1)
    lower = (cc < rr).astype(BF16)
    before = _dot(lower, onehot.astype(BF16)) + cnt_scr[...]
    rank_acc = jnp.zeros((tm, LANE), F32)
    for k in range(TOP_K):
        rk = jnp.sum(jnp.where(sels[k], before, 0.0), axis=-1, keepdims=True)
        rank_acc = jnp.where(lane == k, rk, rank_acc)
    rank_out[...] = rank_acc.astype(jnp.int32)
    cnt_scr[...] = cnt_scr[...] + jnp.sum(onehot, axis=0, keepdims=True)
    cnt_out[...] = cnt_scr[...]


def _post_attention(oa, ob, proj, x2, wpa, wpb, wo, modv, g2, rw, rb, seq):
    n, d = x2.shape
    tm = min(512, seq)
    tiles_per_batch = seq // tm
    row = lambda i: (i, 0)
    const = lambda i: (0, 0)
    kern = functools.partial(_post_kernel, tm=tm)
    return pl.pallas_call(
        kern,
        grid=(n // tm,),
        in_specs=[pl.BlockSpec((tm, d), row), pl.BlockSpec((tm, d), row),
                  pl.BlockSpec((tm, d), lambda i: (i, 6)), pl.BlockSpec((tm, d), lambda i: (i, 7)),
                  pl.BlockSpec((tm, d), row),
                  pl.BlockSpec((d, d), const), pl.BlockSpec((d, d), const), pl.BlockSpec((d, d), const),
                  pl.BlockSpec((1, 8, d), lambda i: (i // tiles_per_batch, 0, 0)),
                  pl.BlockSpec((1, d), const),
                  pl.BlockSpec((d, LANE), const), pl.BlockSpec((1, LANE), const)],
        out_specs=[pl.BlockSpec((tm, d), row), pl.BlockSpec((tm, d), row),
                   pl.BlockSpec((tm, LANE), row), pl.BlockSpec((tm, LANE), row),
                   pl.BlockSpec((tm, LANE), row), pl.BlockSpec((1, LANE), const)],
        out_shape=[jax.ShapeDtypeStruct((n, d), F32), jax.ShapeDtypeStruct((n, d), F32),
                   jax.ShapeDtypeStruct((n, LANE), jnp.int32), jax.ShapeDtypeStruct((n, LANE), F32),
                   jax.ShapeDtypeStruct((n, LANE), jnp.int32), jax.ShapeDtypeStruct((1, LANE), F32)],
        scratch_shapes=[pltpu.VMEM((1, LANE), F32)],
        compiler_params=_params(("arbitrary",)),
        name="post_attn",
    )(oa, ob, proj, proj, x2, wpa, wpb, wo, modv, g2, rw, rb)


def _row_copy(src_ref, src_row, dst_ref, dst_row, sem):
    return pltpu.make_async_copy(src_ref.at[pl.ds(src_row, 1)], dst_ref.at[pl.ds(dst_row, 1)], sem)


def _scatter_kernel(dest_ref, h_ref, init_ref, xs_ref, sem, *, tm):
    del init_ref
    base = pl.program_id(0) * (tm * TOP_K)

    def issue(j, _):
        _row_copy(h_ref, j // TOP_K, xs_ref, dest_ref[base + j], sem).start()
        return 0

    lax.fori_loop(0, tm * TOP_K, issue, 0)

    def drain(j, _):
        _row_copy(h_ref, j // TOP_K, xs_ref, dest_ref[base + j], sem).wait()
        return 0

    lax.fori_loop(0, tm * TOP_K, drain, 0)


def _scatter_rows(dest, h2, n_slots):
    n, d = h2.shape
    tm = 256
    kern = functools.partial(_scatter_kernel, tm=tm)
    return pl.pallas_call(
        kern,
        grid_spec=pltpu.PrefetchScalarGridSpec(
            num_scalar_prefetch=1,
            grid=(n // tm,),
            in_specs=[pl.BlockSpec((tm, d), lambda i, dest: (i, 0)),
                      pl.BlockSpec(memory_space=pl.ANY)],
            out_specs=pl.BlockSpec(memory_space=pl.ANY),
            scratch_shapes=[pltpu.SemaphoreType.DMA]),
        out_shape=jax.ShapeDtypeStruct((n_slots, d), F32),
        input_output_aliases={2: 0},
        compiler_params=_params(("arbitrary",)),
        name="moe_scatter",
    )(dest, h2, jnp.zeros((n_slots, d), F32))


def _expert_kernel(bexp_ref, nused_ref, xs_ref, wgu_ref, bgu_ref, wd_ref, bd_ref, y_ref,
                   wgu_bf, wd_bf, *, f):
    b = pl.program_id(0)
    active = b < nused_ref[0]

    @pl.when(active)
    def _():
        prev = bexp_ref[jnp.maximum(b - 1, 0)]

        @pl.when(jnp.logical_or(b == 0, bexp_ref[b] != prev))
        def _():
            wgu_bf[...] = wgu_ref[0].astype(BF16)
            wd_bf[...] = wd_ref[0].astype(BF16)

        gu = _dot(xs_ref[...].astype(BF16), wgu_bf[...]) + bgu_ref[0]
        g = jnp.minimum(gu[:, :f], SWIGLU_LIMIT)
        u = jnp.clip(gu[:, f:], -SWIGLU_LIMIT, SWIGLU_LIMIT)
        act = (u + 1.0) * (g * jax.nn.sigmoid(SWIGLU_ALPHA * g))
        y_ref[...] = _dot(act.astype(BF16), wd_bf[...]) + bd_ref[0]

    @pl.when(jnp.logical_not(active))
    def _():
        y_ref[...] = jnp.zeros_like(y_ref)


def _experts(blk_exp, n_used, xs, w_gu, b_gu, w_down, b_down):
    n_slots, d = xs.shape
    e, _, f2 = w_gu.shape
    f = f2 // 2
    n_blocks = n_slots // EXPERT_BLOCK
    kern = functools.partial(_expert_kernel, f=f)

    def blk(b, bexp, nused):
        return jnp.minimum(b, nused[0] - 1)

    return pl.pallas_call(
        kern,
        grid_spec=pltpu.PrefetchScalarGridSpec(
            num_scalar_prefetch=2,
            grid=(n_blocks,),
            in_specs=[pl.BlockSpec((EXPERT_BLOCK, d), lambda b, bexp, nused: (blk(b, bexp, nused), 0)),
                      pl.BlockSpec((1, d, f2), lambda b, bexp, nused: (bexp[blk(b, bexp, nused)], 0, 0)),
                      pl.BlockSpec((1, 1, f2), lambda b, bexp, nused: (bexp[blk(b, bexp, nused)], 0, 0)),
                      pl.BlockSpec((1, f, d), lambda b, bexp, nused: (bexp[blk(b, bexp, nused)], 0, 0)),
                      pl.BlockSpec((1, 1, d), lambda b, bexp, nused: (bexp[blk(b, bexp, nused)], 0, 0))],
            out_specs=pl.BlockSpec((EXPERT_BLOCK, d), lambda b, bexp, nused: (b, 0)),
            scratch_shapes=[pltpu.VMEM((d, f2), BF16), pltpu.VMEM((f, d), BF16)]),
        out_shape=jax.ShapeDtypeStruct((n_slots, d), F32),
        compiler_params=_params(("arbitrary",)),
        name="moe_experts",
    )(blk_exp, n_used, xs, w_gu, b_gu.reshape(e, 1, f2), w_down, b_down.reshape(e, 1, d))


def _combine_kernel(dest_ref, y_ref, gate_ref, x_ref, mod_ref, fg_ref, o_ref, buf, sem, *, tm, final):
    base = pl.program_id(0) * (tm * TOP_K)

    def issue(j, _):
        _row_copy(y_ref, dest_ref[base + j], buf.at[j % TOP_K], j // TOP_K, sem).start()
        return 0

    lax.fori_loop(0, tm * TOP_K, issue, 0)

    def drain(j, _):
        _row_copy(y_ref, dest_ref[base + j], buf.at[j % TOP_K], j // TOP_K, sem).wait()
        return 0

    lax.fori_loop(0, tm * TOP_K, drain, 0)

    gate = gate_ref[...]
    moe = gate[:, 0:1] * buf[0]
    for k in range(1, TOP_K):
        moe = moe + gate[:, k:k + 1] * buf[k]
    x = x_ref[...] + mod_ref[0, 5:6, :] * moe
    if final:
        x = (x * lax.rsqrt(jnp.mean(x * x, axis=-1, keepdims=True) + EPS)) * fg_ref[...]
    o_ref[...] = x


def _combine(dest, y, gates, x2, modv, final_g, seq, final):
    n, d = x2.shape
    tm = 256
    tiles_per_batch = seq // tm
    kern = functools.partial(_combine_kernel, tm=tm, final=final)
    return pl.pallas_call(
        kern,
        grid_spec=pltpu.PrefetchScalarGridSpec(
            num_scalar_prefetch=1,
            grid=(n // tm,),
            in_specs=[pl.BlockSpec(memory_space=pl.ANY),
                      pl.BlockSpec((tm, LANE), lambda i, dest: (i, 0)),
                      pl.BlockSpec((tm, d), lambda i, dest: (i, 0)),
                      pl.BlockSpec((1, 8, d), lambda i, dest: (i // tiles_per_batch, 0, 0)),
                      pl.BlockSpec((1, d), lambda i, dest: (0, 0))],
            out_specs=pl.BlockSpec((tm, d), lambda i, dest: (i, 0)),
            scratch_shapes=[pltpu.VMEM((TOP_K, tm, d), F32), pltpu.SemaphoreType.DMA]),
        out_shape=jax.ShapeDtypeStruct((n, d), F32),
        compiler_params=_params(("arbitrary",)),
        name="moe_combine",
    )(dest, y, gates, x2, modv, final_g)


def _routing_tables(counts_f, eidx, rank, n_blocks):
    counts = counts_f[0, :N_EXPERTS].astype(jnp.int32)
    padded = ((counts + EXPERT_BLOCK - 1) // EXPERT_BLOCK) * EXPERT_BLOCK
    pad_end = jnp.cumsum(padded)
    pad_start = pad_end - padded
    dest = (pad_start[eidx] + rank).reshape(-1).astype(jnp.int32)
    blk_exp = jnp.minimum(
        jnp.searchsorted(pad_end, jnp.arange(n_blocks, dtype=jnp.int32) * EXPERT_BLOCK, side='right'),
        N_EXPERTS - 1).astype(jnp.int32)
    n_used = (pad_end[-1:] // EXPERT_BLOCK).astype(jnp.int32)
    return dest, blk_exp, n_used


def kernel(x, c, w_ada, b_ada, norm1_g, w_in, lam_q1, lam_k1, lam_q2, lam_k2, subln_g, w_pa, w_pb,
           w_o, norm2_g, router_w, router_b, w_gu, b_gu, w_down, b_down, final_g):
    batch, seq, d = x.shape
    depth = w_ada.shape[0]
    n = batch * seq
    n_blocks = -(-(n * TOP_K) // EXPERT_BLOCK) + N_EXPERTS
    n_slots = n_blocks * EXPERT_BLOCK

    c8 = jnp.zeros((8, d), F32).at[:batch].set(c)
    mod = _ada_mod(c8, w_ada, b_ada)
    slopes = jnp.asarray([2.0 ** (-8.0 * (i + 1) / DA_HEADS) for i in range(DA_HEADS)], F32)
    fg = final_g.reshape(1, d)

    x2 = x.reshape(n, d)
    for l in range(depth):
        modv = jnp.zeros((batch, 8, d), F32).at[:, :6].set(mod[l, :batch].reshape(batch, 6, d))
        lam_init = 0.8 - 0.6 * math.exp(-0.3 * l)
        proj = _inproj(x2, modv, norm1_g[l].reshape(1, d), w_in[l].astype(BF16), seq)
        lamp = jnp.stack([lam_q1[l], lam_k1[l], lam_q2[l], lam_k2[l]])
        oa = _da_attention(proj, slopes, lamp, subln_g[l].reshape(1, LANE), batch, seq, lam_init)
        ob = _sb_attention(proj, batch, seq)
        rw = jnp.zeros((d, LANE), F32).at[:, :N_EXPERTS].set(router_w[l])
        rb = jnp.full((1, LANE), NEG, F32).at[0, :N_EXPERTS].set(router_b[l])
        x2, h2, eidx, gates, rank, counts = _post_attention(
            oa, ob, proj, x2, w_pa[l].astype(BF16), w_pb[l].astype(BF16), w_o[l].astype(BF16),
            modv, norm2_g[l].reshape(1, d), rw, rb, seq)
        dest, blk_exp, n_used = _routing_tables(counts, eidx[:, :TOP_K], rank[:, :TOP_K], n_blocks)
        xs = _scatter_rows(dest, h2, n_slots)
        y = _experts(blk_exp, n_used, xs, w_gu[l], b_gu[l], w_down[l], b_down[l])
        x2 = _combine(dest, y, gates, x2, modv, fg, seq, final=(l == depth - 1))
    return x2.reshape(batch, seq, d)
```

```python
import functools
import math

import jax
import jax.numpy as jnp
from jax import lax
from jax.experimental import pallas as pl
from jax.experimental.pallas import tpu as pltpu

F32 = jnp.float32
BF16 = jnp.bfloat16

LANE = 128
SUBLANE = 8
VMEM_LIMIT = 56 * 1024 * 1024

CHUNK = 64
DA_HEADS = 8
DA_HEAD_DIM = 64
SB_HEADS = 16
SB_HEAD_DIM = 64
N_EXPERTS = 32
TOP_K = 4
SWIGLU_LIMIT = 7.0
SWIGLU_ALPHA = 1.702
EPS = 1e-6

NEG = -1e30
F32_EXP_ZERO = 104.0

EXPERT_BLOCK = 256
PAD_BITS = EXPERT_BLOCK.bit_length() - 1

DA_QK = DA_HEADS * DA_HEAD_DIM
DA_Q_COLS = (0, 2 * DA_QK)
SB_Q_COLS = (4 * DA_QK + DA_HEADS * 2 * DA_HEAD_DIM,
             4 * DA_QK + DA_HEADS * 2 * DA_HEAD_DIM + SB_HEADS * SB_HEAD_DIM)
QK_SCALE = 1.0 / math.sqrt(DA_HEAD_DIM)
assert DA_HEAD_DIM == SB_HEAD_DIM


def _params(sem, vmem=VMEM_LIMIT):
    return pltpu.CompilerParams(dimension_semantics=sem, vmem_limit_bytes=vmem)


def _dot_nt(a, b):
    return lax.dot_general(a, b, (((1,), (1,)), ((), ())), preferred_element_type=F32)


def _dot(a, b):
    return jnp.dot(a, b, preferred_element_type=F32)


def _ada_kernel(c_ref, w_ref, b_ref, o_ref):
    o_ref[0] = jnp.dot(c_ref[...], w_ref[0], preferred_element_type=F32,
                       precision=lax.Precision.HIGHEST) + b_ref[0]


def _ada_mod(c8, w_ada, b_ada):
    depth, d, d6 = w_ada.shape
    tn = 2048
    return pl.pallas_call(
        _ada_kernel,
        grid=(depth, d6 // tn),
        in_specs=[pl.BlockSpec((8, d), lambda l, j: (0, 0)),
                  pl.BlockSpec((1, d, tn), lambda l, j: (l, 0, j)),
                  pl.BlockSpec((1, 1, tn), lambda l, j: (l, 0, j))],
        out_specs=pl.BlockSpec((1, 8, tn), lambda l, j: (l, 0, j)),
        out_shape=jax.ShapeDtypeStruct((depth, 8, d6), F32),
        compiler_params=_params(("parallel", "parallel")),
        name="ada_mod",
    )(c8, w_ada, b_ada.reshape(depth, 1, d6))


def _rms_mod(x, g, sc, sh):
    y = x * lax.rsqrt(jnp.mean(x * x, axis=-1, keepdims=True) + EPS)
    return (y * g) * (1.0 + sc) + sh


def _inproj_kernel(x_ref, mod_ref, g_ref, w_ref, o_ref, h_scr, *, tn):
    @pl.when(pl.program_id(1) == 0)
    def _():
        h = _rms_mod(x_ref[...], g_ref[...], mod_ref[0, 1:2, :], mod_ref[0, 0:1, :])
        h_scr[...] = h.astype(BF16)

    j = pl.program_id(1)
    is_q = jnp.logical_or(j == DA_Q_COLS[0] // tn, j == SB_Q_COLS[0] // tn)
    scale = jnp.where(is_q, QK_SCALE, 1.0)
    o_ref[...] = (_dot(h_scr[...], w_ref[...]) * scale).astype(BF16)


def _inproj(x2, modv, g, w_bf, seq):
    n, d = x2.shape
    cols = w_bf.shape[1]
    tm = min(1024, seq)
    tn = 1024
    assert DA_Q_COLS == (0, tn) and SB_Q_COLS[0] % tn == 0 and SB_Q_COLS[1] - SB_Q_COLS[0] == tn
    tiles_per_batch = seq // tm
    return pl.pallas_call(
        functools.partial(_inproj_kernel, tn=tn),
        grid=(n // tm, cols // tn),
        in_specs=[pl.BlockSpec((tm, d), lambda i, j: (i, 0)),
                  pl.BlockSpec((1, 8, d), lambda i, j: (i // tiles_per_batch, 0, 0)),
                  pl.BlockSpec((1, d), lambda i, j: (0, 0)),
                  pl.BlockSpec((d, tn), lambda i, j: (0, j))],
        out_specs=pl.BlockSpec((tm, tn), lambda i, j: (i, j)),
        out_shape=jax.ShapeDtypeStruct((n, cols), BF16),
        scratch_shapes=[pltpu.VMEM((tm, d), BF16)],
        compiler_params=_params(("parallel", "arbitrary")),
        name="inproj",
    )(x2, modv, g, w_bf)


def _da_kernel(slopes_ref, q1_ref, q2_ref, k1_ref, k2_ref, v_ref, lamp_ref, subg_ref, o_ref,
               vt_scr, rowb_scr, *, tq, lam_init):
    h = pl.program_id(1)
    qi = pl.program_id(2)
    slope = slopes_ref[h]
    seq = v_ref.shape[0]

    @pl.when(qi == 0)
    def _():
        def tr(i, _):
            k0 = pl.multiple_of(i * tq, tq)
            vt_scr[:, pl.ds(k0, tq)] = v_ref[pl.ds(k0, tq), :].astype(F32).T.astype(BF16)
            return 0
        lax.fori_loop(0, seq // tq, tr, 0)

    lane = lax.broadcasted_iota(jnp.int32, (1, LANE), 1)
    mk = (lane // DA_HEAD_DIM) == (h % 2)
    zero = jnp.zeros((), BF16)
    q1 = jnp.where(mk, q1_ref[...], zero)
    q2 = jnp.where(mk, q2_ref[...], zero)
    q0 = qi * tq
    r = lax.broadcasted_iota(jnp.int32, (tq, tq), 0)
    c = lax.broadcasted_iota(jnp.int32, (tq, tq), 1)
    rowb_scr[...] = slope * r.astype(F32)

    def update(sb, cst, m, l, acc, vt):
        m_new = jnp.maximum(m, jnp.max(sb, axis=0, keepdims=True) + cst)
        alpha = jnp.exp(m - m_new)
        p = jnp.exp(sb - (m_new - cst))
        l = alpha * l + jnp.sum(p, axis=0, keepdims=True)
        acc = alpha * acc + _dot(vt, p.astype(BF16))
        return m_new, l, acc

    def off_body(kb, carry):
        m1, l1, a1, m2, l2, a2 = carry
        k0 = pl.multiple_of(kb * tq, tq)
        vt = vt_scr[:, pl.ds(k0, tq)]
        cst = slope * (k0 - q0).astype(F32)
        s1 = _dot_nt(k1_ref[pl.ds(k0, tq), :], q1) + rowb_scr[...]
        m1, l1, a1 = update(s1, cst, m1, l1, a1, vt)
        s2 = _dot_nt(k2_ref[pl.ds(k0, tq), :], q2) + rowb_scr[...]
        m2, l2, a2 = update(s2, cst, m2, l2, a2, vt)
        return m1, l1, a1, m2, l2, a2

    m0 = jnp.full((1, tq), NEG, F32)
    l0 = jnp.zeros((1, tq), F32)
    a0 = jnp.zeros((LANE, tq), F32)
    m1, l1, a1, m2, l2, a2 = lax.fori_loop(0, qi, off_body, (m0, l0, a0, m0, l0, a0))

    allowed = (r // CHUNK) <= (c // CHUNK)
    bias = slope * (c - jnp.abs(c - r)).astype(F32)
    k0 = pl.multiple_of(q0, tq)
    vt = vt_scr[:, pl.ds(k0, tq)]
    s1 = jnp.where(allowed, _dot_nt(k1_ref[pl.ds(k0, tq), :], q1) + bias, NEG)
    m1, l1, a1 = update(s1, 0.0, m1, l1, a1, vt)
    s2 = jnp.where(allowed, _dot_nt(k2_ref[pl.ds(k0, tq), :], q2) + bias, NEG)
    m2, l2, a2 = update(s2, 0.0, m2, l2, a2, vt)

    lp = lamp_ref[...]
    lam = (jnp.exp(jnp.sum(lp[0:1] * lp[1:2], axis=-1, keepdims=True))
           - jnp.exp(jnp.sum(lp[2:3] * lp[3:4], axis=-1, keepdims=True)) + lam_init)
    o = a1 / l1 - lam * (a2 / l2)
    y = o * lax.rsqrt(jnp.mean(o * o, axis=0, keepdims=True) + EPS)
    o_ref[...] = ((y.T * subg_ref[...]) * (1.0 - lam_init)).astype(BF16)


def _da_attention(proj, slopes, lamp, subg, batch, seq, lam_init):
    n = proj.shape[0]
    tq = min(512, seq)
    nq = seq // tq
    kern = functools.partial(_da_kernel, tq=tq, lam_init=lam_init)
    qspec = lambda off: pl.BlockSpec((tq, LANE), lambda b, h, i: (b * nq + i, off + h // 2))
    kspec = lambda off: pl.BlockSpec((seq, LANE), lambda b, h, i: (b, off + h // 2))
    return pl.pallas_call(
        kern,
        grid=(batch, DA_HEADS, nq),
        in_specs=[pl.BlockSpec(memory_space=pltpu.SMEM),
                  qspec(0), qspec(4), kspec(8), kspec(12),
                  pl.BlockSpec((seq, LANE), lambda b, h, i: (b, 16 + h)),
                  pl.BlockSpec((4, DA_HEAD_DIM), lambda b, h, i: (0, 0)),
                  pl.BlockSpec((1, LANE), lambda b, h, i: (0, 0))],
        out_specs=pl.BlockSpec((tq, LANE), lambda b, h, i: (b * nq + i, h)),
        out_shape=jax.ShapeDtypeStruct((n, DA_HEADS * 2 * DA_HEAD_DIM), BF16),
        scratch_shapes=[pltpu.VMEM((LANE, seq), BF16), pltpu.VMEM((tq, tq), F32)],
        compiler_params=_params(("parallel", "parallel", "arbitrary")),
        name="da_attn",
    )(slopes, proj, proj, proj, proj, proj, lamp, subg)


def _sb_kernel(q_ref, k_ref, v_ref, o_ref, acc_scr, rsum_scr, *, tqb, ts):
    qi = pl.program_id(2)
    nsub = tqb // ts
    lane = lax.broadcasted_iota(jnp.int32, (1, LANE), 1)
    r = lax.broadcasted_iota(jnp.int32, (ts, ts), 0)
    c = lax.broadcasted_iota(jnp.int32, (ts, ts), 1)
    upper = (r > c).astype(BF16)
    strict = c < r
    zero = jnp.zeros((), BF16)
    head_mask = [(lane // SB_HEAD_DIM) == hs for hs in range(2)]

    acc_scr[...] = jnp.zeros_like(acc_scr)
    rsum_scr[...] = jnp.zeros_like(rsum_scr)

    def cond(carry):
        return carry[1] == 0

    def body(carry):
        j, _ = carry
        alive = jnp.full((ts, 1), 2.0 * F32_EXP_ZERO, F32)
        for sub in range(nsub):
            kb = qi * nsub + sub - j
            k0 = pl.multiple_of(jnp.maximum(kb, 0) * ts, ts)
            k = k_ref[pl.ds(k0, ts), :]
            v = v_ref[pl.ds(k0, ts), :]
            valid = jnp.logical_and(jnp.logical_or(strict, j > 0), kb >= 0)
            for hs in range(2):
                s = sub * 2 + hs
                q = jnp.where(head_mask[hs], q_ref[sub * ts:(sub + 1) * ts, :], zero)
                z = _dot_nt(q, k)
                t = jnp.log1p(jnp.exp(-jnp.abs(z)))
                sp = jnp.where(valid, jnp.maximum(z, 0.0) + t, 0.0)
                logsig = jnp.minimum(z, 0.0) - t
                hi = sp.astype(BF16)
                lo = (sp - hi.astype(F32)).astype(BF16)
                later = _dot(hi, upper) + _dot(lo, upper)
                rsum = rsum_scr[s]
                a = jnp.where(valid, jnp.exp(logsig - (rsum + later)), 0.0)
                acc_scr[s] += _dot(a.astype(BF16), v)
                rsum = rsum + jnp.sum(sp, axis=-1, keepdims=True)
                rsum_scr[s] = rsum
                alive = jnp.minimum(alive, jnp.where(kb > 0, rsum, 2.0 * F32_EXP_ZERO))
        done = (jnp.min(alive) > F32_EXP_ZERO).astype(jnp.int32)
        return j + 1, done

    lax.while_loop(cond, body, (jnp.int32(0), jnp.int32(0)))
    for sub in range(nsub):
        o_ref[sub * ts:(sub + 1) * ts, :] = jnp.where(
            head_mask[0], acc_scr[2 * sub], acc_scr[2 * sub + 1]).astype(BF16)


def _sb_attention(proj, batch, seq):
    n = proj.shape[0]
    tqb = min(512, seq)
    ts = 128
    nq = seq // tqb
    pairs = SB_HEADS // 2
    kern = functools.partial(_sb_kernel, tqb=tqb, ts=ts)
    return pl.pallas_call(
        kern,
        grid=(batch, pairs, nq),
        in_specs=[pl.BlockSpec((tqb, LANE), lambda b, p, i: (b * nq + i, 24 + p)),
                  pl.BlockSpec((seq, LANE), lambda b, p, i: (b, 32 + p)),
                  pl.BlockSpec((seq, LANE), lambda b, p, i: (b, 40 + p))],
        out_specs=pl.BlockSpec((tqb, LANE), lambda b, p, i: (b * nq + i, p)),
        out_shape=jax.ShapeDtypeStruct((n, SB_HEADS * SB_HEAD_DIM), BF16),
        scratch_shapes=[pltpu.VMEM((2 * (tqb // ts), ts, LANE), F32),
                        pltpu.VMEM((2 * (tqb // ts), ts, 1), F32)],
        compiler_params=_params(("parallel", "parallel", "arbitrary")),
        name="sb_attn",
    )(proj, proj, proj)


def _post_kernel(oa_ref, ob_ref, ga_ref, gb_ref, x_ref, wpa_ref, wpb_ref, wo_ref, mod_ref, g2_ref,
                 rw_ref, rb_ref, x_out, h_out, idx_out, gate_out, rank_out, cnt_out, cnt_scr, *, tm):
    @pl.when(pl.program_id(0) == 0)
    def _():
        cnt_scr[...] = jnp.zeros_like(cnt_scr)

    pa = _dot(oa_ref[...], wpa_ref[...])
    pb = _dot(ob_ref[...], wpb_ref[...])
    merged = (jax.nn.sigmoid(ga_ref[...].astype(F32)) * pa
              + jax.nn.sigmoid(gb_ref[...].astype(F32)) * pb)
    att = _dot(merged.astype(BF16), wo_ref[...])
    x = x_ref[...] + mod_ref[0, 2:3, :] * att
    x_out[...] = x
    h = _rms_mod(x, g2_ref[...], mod_ref[0, 4:5, :], mod_ref[0, 3:4, :])
    h_out[...] = h

    logits = jnp.dot(h, rw_ref[...], preferred_element_type=F32,
                     precision=lax.Precision.HIGHEST) + rb_ref[...]
    lane = lax.broadcasted_iota(jnp.int32, (1, LANE), 1)
    lane_f = lane.astype(F32)
    work = logits
    idx_acc = jnp.zeros((tm, LANE), F32)
    val_acc = jnp.zeros((tm, LANE), F32)
    onehot = jnp.zeros((tm, LANE), F32)
    sels = []
    top0 = None
    for k in range(TOP_K):
        mx = jnp.max(work, axis=-1, keepdims=True)
        sel = jnp.min(jnp.where(work == mx, lane_f, float(LANE)), axis=-1, keepdims=True)
        hit = lane_f == sel
        if k == 0:
            top0 = mx
        idx_acc = jnp.where(lane == k, sel, idx_acc)
        val_acc = jnp.where(lane == k, jnp.exp(mx - top0), val_acc)
        onehot = onehot + hit.astype(F32)
        work = jnp.where(hit, NEG * 2.0, work)
        sels.append(hit)
    gate_out[...] = val_acc / jnp.sum(val_acc, axis=-1, keepdims=True)
    idx_out[...] = idx_acc.astype(jnp.int32)

    rr = lax.broadcasted_iota(jnp.int32, (tm, tm), 0)
    cc = lax.broadcasted_iota(jnp.int32, (tm, tm), 1)
    lower = (cc < rr).astype(BF16)
    before = _dot(lower, onehot.astype(BF16)) + cnt_scr[...]
    rank_acc = jnp.zeros((tm, LANE), F32)
    for k in range(TOP_K):
        rk = jnp.sum(jnp.where(sels[k], before, 0.0), axis=-1, keepdims=True)
        rank_acc = jnp.where(lane == k, rk, rank_acc)
    rank_out[...] = rank_acc.astype(jnp.int32)
    cnt_scr[...] = cnt_scr[...] + jnp.sum(onehot, axis=0, keepdims=True)
    cnt_out[...] = cnt_scr[...]


def _post_attention(oa, ob, proj, x2, wpa, wpb, wo, modv, g2, rw, rb, seq):
    n, d = x2.shape
    tm = min(512, seq)
    tiles_per_batch = seq // tm
    row = lambda i: (i, 0)
    const = lambda i: (0, 0)
    kern = functools.partial(_post_kernel, tm=tm)
    return pl.pallas_call(
        kern,
        grid=(n // tm,),
        in_specs=[pl.BlockSpec((tm, d), row), pl.BlockSpec((tm, d), row),
                  pl.BlockSpec((tm, d), lambda i: (i, 6)), pl.BlockSpec((tm, d), lambda i: (i, 7)),
                  pl.BlockSpec((tm, d), row),
                  pl.BlockSpec((d, d), const), pl.BlockSpec((d, d), const), pl.BlockSpec((d, d), const),
                  pl.BlockSpec((1, 8, d), lambda i: (i // tiles_per_batch, 0, 0)),
                  pl.BlockSpec((1, d), const),
                  pl.BlockSpec((d, LANE), const), pl.BlockSpec((1, LANE), const)],
        out_specs=[pl.BlockSpec((tm, d), row), pl.BlockSpec((tm, d), row),
                   pl.BlockSpec((tm, LANE), row), pl.BlockSpec((tm, LANE), row),
                   pl.BlockSpec((tm, LANE), row), pl.BlockSpec((1, LANE), const)],
        out_shape=[jax.ShapeDtypeStruct((n, d), F32), jax.ShapeDtypeStruct((n, d), F32),
                   jax.ShapeDtypeStruct((n, LANE), jnp.int32), jax.ShapeDtypeStruct((n, LANE), F32),
                   jax.ShapeDtypeStruct((n, LANE), jnp.int32), jax.ShapeDtypeStruct((1, LANE), F32)],
        scratch_shapes=[pltpu.VMEM((1, LANE), F32)],
        compiler_params=_params(("arbitrary",)),
        name="post_attn",
    )(oa, ob, proj, proj, x2, wpa, wpb, wo, modv, g2, rw, rb)


ROWS_PER_ISSUE = 8


def _row_copy(src_ref, src_row, dst_ref, dst_row, sem):
    return pltpu.make_async_copy(src_ref.at[pl.ds(src_row, 1)], dst_ref.at[pl.ds(dst_row, 1)], sem)


def _scatter_kernel(dest_ref, zoff_ref, zcnt_ref, h_ref, xs_ref, zbuf, sem, zsem, *, tm):
    i = pl.program_id(0)
    base = i * (tm * TOP_K)

    def row_copies(g, fn):
        for rr in range(ROWS_PER_ISSUE):
            row = g * ROWS_PER_ISSUE + rr
            for k in range(TOP_K):
                fn(_row_copy(h_ref, row, xs_ref, dest_ref[base + row * TOP_K + k], sem))

    def issue(g, _):
        row_copies(g, lambda cp: cp.start())
        return 0

    lax.fori_loop(0, tm // ROWS_PER_ISSUE, issue, 0)

    @pl.when(i == 0)
    def _():
        zbuf[...] = jnp.zeros_like(zbuf)

        def pad_copies(e, fn):
            cnt = zcnt_ref[e]
            off = zoff_ref[e]
            n_single = jnp.minimum((SUBLANE - (off & (SUBLANE - 1))) & (SUBLANE - 1), cnt)
            for s in range(SUBLANE - 1):
                @pl.when(s < n_single)
                def _():
                    fn(pltpu.make_async_copy(zbuf.at[pl.ds(0, 1)], xs_ref.at[pl.ds(off + s, 1)], zsem))

            off8 = off + n_single
            rem = cnt - n_single
            for bit in range(PAD_BITS - 1, SUBLANE.bit_length() - 2, -1):
                size = 1 << bit

                @pl.when(((rem >> bit) & 1) == 1)
                def _():
                    start = pl.multiple_of(off8 + ((rem >> (bit + 1)) << (bit + 1)), SUBLANE)
                    fn(pltpu.make_async_copy(zbuf.at[pl.ds(0, size)], xs_ref.at[pl.ds(start, size)], zsem))

        def pad_issue(e, _):
            pad_copies(e, lambda cp: cp.start())
            return 0

        def pad_drain(e, _):
            pad_copies(e, lambda cp: cp.wait())
            return 0

        lax.fori_loop(0, N_EXPERTS, pad_issue, 0)
        lax.fori_loop(0, N_EXPERTS, pad_drain, 0)

        zrows = zbuf.shape[0]

        def tail_copy(t):
            start = pl.multiple_of(t * zrows, zrows)
            return pltpu.make_async_copy(zbuf, xs_ref.at[pl.ds(start, zrows)], zsem)

        def tail_issue(t, _):
            tail_copy(t).start()
            return 0

        def tail_drain(t, _):
            tail_copy(t).wait()
            return 0

        first_tail = zoff_ref[N_EXPERTS] // zrows
        lax.fori_loop(first_tail, xs_ref.shape[0] // zrows, tail_issue, 0)
        lax.fori_loop(first_tail, xs_ref.shape[0] // zrows, tail_drain, 0)

    def drain(g, _):
        row_copies(g, lambda cp: cp.wait())
        return 0

    lax.fori_loop(0, tm // ROWS_PER_ISSUE, drain, 0)


def _scatter_rows(dest, zoff, zcnt, h2, n_slots):
    n, d = h2.shape
    tm = 256
    kern = functools.partial(_scatter_kernel, tm=tm)
    return pl.pallas_call(
        kern,
        grid_spec=pltpu.PrefetchScalarGridSpec(
            num_scalar_prefetch=3,
            grid=(n // tm,),
            in_specs=[pl.BlockSpec((tm, d), lambda i, *_: (i, 0))],
            out_specs=pl.BlockSpec(memory_space=pl.ANY),
            scratch_shapes=[pltpu.VMEM((1 << (PAD_BITS - 1), d), F32),
                            pltpu.SemaphoreType.DMA, pltpu.SemaphoreType.DMA]),
        out_shape=jax.ShapeDtypeStruct((n_slots, d), F32),
        compiler_params=_params(("arbitrary",)),
        name="moe_scatter",
    )(dest, zoff, zcnt, h2)


def _expert_kernel(bexp_ref, nused_ref, xs_ref, wgu_ref, bgu_ref, wd_ref, bd_ref, y_ref,
                   wgu_bf, wd_bf, *, f):
    b = pl.program_id(0)
    active = b < nused_ref[0]

    @pl.when(active)
    def _():
        prev = bexp_ref[jnp.maximum(b - 1, 0)]

        @pl.when(jnp.logical_or(b == 0, bexp_ref[b] != prev))
        def _():
            wgu_bf[...] = wgu_ref[0].astype(BF16)
            wd_bf[...] = wd_ref[0].astype(BF16)

        gu = _dot(xs_ref[...].astype(BF16), wgu_bf[...]) + bgu_ref[0]
        g = jnp.minimum(gu[:, :f], SWIGLU_LIMIT)
        u = jnp.clip(gu[:, f:], -SWIGLU_LIMIT, SWIGLU_LIMIT)
        act = (u + 1.0) * (g * jax.nn.sigmoid(SWIGLU_ALPHA * g))
        y_ref[...] = _dot(act.astype(BF16), wd_bf[...]) + bd_ref[0]

    @pl.when(jnp.logical_not(active))
    def _():
        y_ref[...] = jnp.zeros_like(y_ref)


def _experts(blk_exp, n_used, xs, w_gu, b_gu, w_down, b_down, layer):
    n_slots, d = xs.shape
    depth, e, _, f2 = w_gu.shape
    f = f2 // 2
    n_blocks = n_slots // EXPERT_BLOCK
    kern = functools.partial(_expert_kernel, f=f)

    def blk(b, bexp, nused):
        return jnp.minimum(b, nused[0] - 1)

    def wsel(b, bexp, nused):
        return (layer * e + bexp[blk(b, bexp, nused)], 0, 0)

    return pl.pallas_call(
        kern,
        grid_spec=pltpu.PrefetchScalarGridSpec(
            num_scalar_prefetch=2,
            grid=(n_blocks,),
            in_specs=[pl.BlockSpec((EXPERT_BLOCK, d), lambda b, bexp, nused: (blk(b, bexp, nused), 0)),
                      pl.BlockSpec((1, d, f2), wsel), pl.BlockSpec((1, 1, f2), wsel),
                      pl.BlockSpec((1, f, d), wsel), pl.BlockSpec((1, 1, d), wsel)],
            out_specs=pl.BlockSpec((EXPERT_BLOCK, d), lambda b, bexp, nused: (b, 0)),
            scratch_shapes=[pltpu.VMEM((d, f2), BF16), pltpu.VMEM((f, d), BF16)]),
        out_shape=jax.ShapeDtypeStruct((n_slots, d), F32),
        compiler_params=_params(("arbitrary",)),
        name="moe_experts",
    )(blk_exp, n_used, xs, w_gu.reshape(depth * e, d, f2), b_gu.reshape(depth * e, 1, f2),
      w_down.reshape(depth * e, f, d), b_down.reshape(depth * e, 1, d))


def _combine_kernel(dest_ref, y_ref, gate_ref, x_ref, mod_ref, fg_ref, o_ref, buf, sem, *, tm, final):
    base = pl.program_id(0) * (tm * TOP_K)

    def row_copies(g, fn):
        for rr in range(ROWS_PER_ISSUE):
            row = g * ROWS_PER_ISSUE + rr
            for k in range(TOP_K):
                fn(_row_copy(y_ref, dest_ref[base + row * TOP_K + k], buf.at[k], row, sem))

    def issue(g, _):
        row_copies(g, lambda cp: cp.start())
        return 0

    lax.fori_loop(0, tm // ROWS_PER_ISSUE, issue, 0)

    def drain(g, _):
        row_copies(g, lambda cp: cp.wait())
        return 0

    lax.fori_loop(0, tm // ROWS_PER_ISSUE, drain, 0)

    gate = gate_ref[...]
    moe = gate[:, 0:1] * buf[0]
    for k in range(1, TOP_K):
        moe = moe + gate[:, k:k + 1] * buf[k]
    x = x_ref[...] + mod_ref[0, 5:6, :] * moe
    if final:
        x = (x * lax.rsqrt(jnp.mean(x * x, axis=-1, keepdims=True) + EPS)) * fg_ref[...]
    o_ref[...] = x


def _combine(dest, y, gates, x2, modv, final_g, seq, final):
    n, d = x2.shape
    tm = 256
    tiles_per_batch = seq // tm
    kern = functools.partial(_combine_kernel, tm=tm, final=final)
    return pl.pallas_call(
        kern,
        grid_spec=pltpu.PrefetchScalarGridSpec(
            num_scalar_prefetch=1,
            grid=(n // tm,),
            in_specs=[pl.BlockSpec(memory_space=pl.ANY),
                      pl.BlockSpec((tm, LANE), lambda i, dest: (i, 0)),
                      pl.BlockSpec((tm, d), lambda i, dest: (i, 0)),
                      pl.BlockSpec((1, 8, d), lambda i, dest: (i // tiles_per_batch, 0, 0)),
                      pl.BlockSpec((1, d), lambda i, dest: (0, 0))],
            out_specs=pl.BlockSpec((tm, d), lambda i, dest: (i, 0)),
            scratch_shapes=[pltpu.VMEM((TOP_K, tm, d), F32), pltpu.SemaphoreType.DMA]),
        out_shape=jax.ShapeDtypeStruct((n, d), F32),
        compiler_params=_params(("arbitrary",)),
        name="moe_combine",
    )(dest, y, gates, x2, modv, final_g)


def _routing_tables(counts_f, eidx, rank, n_blocks):
    counts = counts_f[0, :N_EXPERTS].astype(jnp.int32)
    padded = ((counts + EXPERT_BLOCK - 1) // EXPERT_BLOCK) * EXPERT_BLOCK
    pad_end = jnp.cumsum(padded)
    pad_start = pad_end - padded
    dest = (pad_start[eidx] + rank).reshape(-1).astype(jnp.int32)
    blk_first = jnp.arange(n_blocks, dtype=jnp.int32) * EXPERT_BLOCK
    blk_exp = jnp.minimum(jnp.sum(pad_end[None, :] <= blk_first[:, None], axis=1),
                          N_EXPERTS - 1).astype(jnp.int32)
    n_used = (pad_end[-1:] // EXPERT_BLOCK).astype(jnp.int32)
    pad_first = jnp.concatenate([pad_start + counts, pad_end[-1:]]).astype(jnp.int32)
    pad_count = (padded - counts).astype(jnp.int32)
    return dest, blk_exp, n_used, pad_first, pad_count


def kernel(x, c, w_ada, b_ada, norm1_g, w_in, lam_q1, lam_k1, lam_q2, lam_k2, subln_g, w_pa, w_pb,
           w_o, norm2_g, router_w, router_b, w_gu, b_gu, w_down, b_down, final_g):
    batch, seq, d = x.shape
    depth = w_ada.shape[0]
    n = batch * seq
    n_blocks = -(-(n * TOP_K) // EXPERT_BLOCK) + N_EXPERTS
    n_slots = n_blocks * EXPERT_BLOCK

    c8 = jnp.zeros((8, d), F32).at[:batch].set(c)
    mod = _ada_mod(c8, w_ada, b_ada)
    slopes = jnp.asarray([2.0 ** (-8.0 * (i + 1) / DA_HEADS) for i in range(DA_HEADS)], F32)
    fg = final_g.reshape(1, d)

    x2 = x.reshape(n, d)
    for l in range(depth):
        modv = jnp.zeros((batch, 8, d), F32).at[:, :6].set(mod[l, :batch].reshape(batch, 6, d))
        lam_init = 0.8 - 0.6 * math.exp(-0.3 * l)
        proj = _inproj(x2, modv, norm1_g[l].reshape(1, d), w_in[l].astype(BF16), seq)
        lamp = jnp.stack([lam_q1[l], lam_k1[l], lam_q2[l], lam_k2[l]])
        oa = _da_attention(proj, slopes, lamp, subln_g[l].reshape(1, LANE), batch, seq, lam_init)
        ob = _sb_attention(proj, batch, seq)
        rw = jnp.zeros((d, LANE), F32).at[:, :N_EXPERTS].set(router_w[l])
        rb = jnp.full((1, LANE), NEG, F32).at[0, :N_EXPERTS].set(router_b[l])
        x2, h2, eidx, gates, rank, counts = _post_attention(
            oa, ob, proj, x2, w_pa[l].astype(BF16), w_pb[l].astype(BF16), w_o[l].astype(BF16),
            modv, norm2_g[l].reshape(1, d), rw, rb, seq)
        dest, blk_exp, n_used, pad_first, pad_count = _routing_tables(
            counts, eidx[:, :TOP_K], rank[:, :TOP_K], n_blocks)
        xs = _scatter_rows(dest, pad_first, pad_count, h2, n_slots)
        y = _experts(blk_exp, n_used, xs, w_gu, b_gu, w_down, b_down, l)
        x2 = _combine(dest, y, gates, x2, modv, fg, seq, final=(l == depth - 1))
    return x2.reshape(batch, seq, d)
```

```python
import functools
import math

import jax
import jax.numpy as jnp
from jax import lax
from jax.experimental import pallas as pl
from jax.experimental.pallas import tpu as pltpu

F32 = jnp.float32
BF16 = jnp.bfloat16

LANE = 128
SUBLANE = 8
VMEM_LIMIT = 56 * 1024 * 1024

CHUNK = 64
DA_HEADS = 8
DA_HEAD_DIM = 64
SB_HEADS = 16
SB_HEAD_DIM = 64
N_EXPERTS = 32
TOP_K = 4
SWIGLU_LIMIT = 7.0
SWIGLU_ALPHA = 1.702
EPS = 1e-6

NEG = -1e30
F32_EXP_ZERO = 104.0

EXPERT_BLOCK = 256
PAD_BITS = EXPERT_BLOCK.bit_length() - 1

DA_QK = DA_HEADS * DA_HEAD_DIM
DA_Q_COLS = (0, 2 * DA_QK)
SB_Q_COLS = (4 * DA_QK + DA_HEADS * 2 * DA_HEAD_DIM,
             4 * DA_QK + DA_HEADS * 2 * DA_HEAD_DIM + SB_HEADS * SB_HEAD_DIM)
QK_SCALE = 1.0 / math.sqrt(DA_HEAD_DIM)
assert DA_HEAD_DIM == SB_HEAD_DIM


def _params(sem, vmem=VMEM_LIMIT):
    return pltpu.CompilerParams(dimension_semantics=sem, vmem_limit_bytes=vmem)


def _dot_nt(a, b):
    return lax.dot_general(a, b, (((1,), (1,)), ((), ())), preferred_element_type=F32)


def _dot(a, b):
    return jnp.dot(a, b, preferred_element_type=F32)


def _ada_kernel(c_ref, w_ref, b_ref, o_ref):
    o_ref[0] = jnp.dot(c_ref[...], w_ref[0], preferred_element_type=F32,
                       precision=lax.Precision.HIGHEST) + b_ref[0]


def _ada_mod(c8, w_ada, b_ada):
    depth, d, d6 = w_ada.shape
    tn = 2048
    return pl.pallas_call(
        _ada_kernel,
        grid=(depth, d6 // tn),
        in_specs=[pl.BlockSpec((8, d), lambda l, j: (0, 0)),
                  pl.BlockSpec((1, d, tn), lambda l, j: (l, 0, j)),
                  pl.BlockSpec((1, 1, tn), lambda l, j: (l, 0, j))],
        out_specs=pl.BlockSpec((1, 8, tn), lambda l, j: (l, 0, j)),
        out_shape=jax.ShapeDtypeStruct((depth, 8, d6), F32),
        compiler_params=_params(("parallel", "parallel")),
        name="ada_mod",
    )(c8, w_ada, b_ada.reshape(depth, 1, d6))


def _rms_mod(x, g, sc, sh):
    y = x * lax.rsqrt(jnp.mean(x * x, axis=-1, keepdims=True) + EPS)
    return (y * g) * (1.0 + sc) + sh


def _inproj_kernel(x_ref, mod_ref, g_ref, w_ref, o_ref, h_scr, *, tn):
    @pl.when(pl.program_id(1) == 0)
    def _():
        h = _rms_mod(x_ref[...], g_ref[...], mod_ref[0, 1:2, :], mod_ref[0, 0:1, :])
        h_scr[...] = h.astype(BF16)

    j = pl.program_id(1)
    is_q = jnp.logical_or(j == DA_Q_COLS[0] // tn, j == SB_Q_COLS[0] // tn)
    scale = jnp.where(is_q, QK_SCALE, 1.0)
    o_ref[...] = (_dot(h_scr[...], w_ref[...]) * scale).astype(BF16)


def _inproj(x2, modv, g, w_bf, seq):
    n, d = x2.shape
    cols = w_bf.shape[1]
    tm = min(1024, seq)
    tn = 1024
    assert DA_Q_COLS == (0, tn) and SB_Q_COLS[0] % tn == 0 and SB_Q_COLS[1] - SB_Q_COLS[0] == tn
    tiles_per_batch = seq // tm
    return pl.pallas_call(
        functools.partial(_inproj_kernel, tn=tn),
        grid=(n // tm, cols // tn),
        in_specs=[pl.BlockSpec((tm, d), lambda i, j: (i, 0)),
                  pl.BlockSpec((1, 8, d), lambda i, j: (i // tiles_per_batch, 0, 0)),
                  pl.BlockSpec((1, d), lambda i, j: (0, 0)),
                  pl.BlockSpec((d, tn), lambda i, j: (0, j))],
        out_specs=pl.BlockSpec((tm, tn), lambda i, j: (i, j)),
        out_shape=jax.ShapeDtypeStruct((n, cols), BF16),
        scratch_shapes=[pltpu.VMEM((tm, d), BF16)],
        compiler_params=_params(("parallel", "arbitrary")),
        name="inproj",
    )(x2, modv, g, w_bf)


def _da_kernel(slopes_ref, q1_ref, q2_ref, k1_ref, k2_ref, v_ref, lamp_ref, subg_ref, o_ref,
               vt_scr, rowb_scr, *, tq, lam_init):
    h = pl.program_id(1)
    qi = pl.program_id(2)
    slope = slopes_ref[h]
    seq = v_ref.shape[0]

    @pl.when(qi == 0)
    def _():
        def tr(i, _):
            k0 = pl.multiple_of(i * tq, tq)
            vt_scr[:, pl.ds(k0, tq)] = v_ref[pl.ds(k0, tq), :].astype(F32).T.astype(BF16)
            return 0
        lax.fori_loop(0, seq // tq, tr, 0)

    lane = lax.broadcasted_iota(jnp.int32, (1, LANE), 1)
    mk = (lane // DA_HEAD_DIM) == (h % 2)
    zero = jnp.zeros((), BF16)
    q1 = jnp.where(mk, q1_ref[...], zero)
    q2 = jnp.where(mk, q2_ref[...], zero)
    q0 = qi * tq
    r = lax.broadcasted_iota(jnp.int32, (tq, tq), 0)
    c = lax.broadcasted_iota(jnp.int32, (tq, tq), 1)
    rowb = slope * r.astype(F32)
    rowb_scr[...] = jnp.concatenate([rowb, rowb], axis=1)

    def scores(k0):
        return jnp.concatenate([_dot_nt(k1_ref[pl.ds(k0, tq), :], q1),
                                _dot_nt(k2_ref[pl.ds(k0, tq), :], q2)], axis=1)

    def update(sb, cst, m, l, acc, vt):
        m_new = jnp.maximum(m, jnp.max(sb, axis=0, keepdims=True) + cst)
        alpha = jnp.exp(m - m_new)
        p = jnp.exp(sb - (m_new - cst))
        l = alpha * l + jnp.sum(p, axis=0, keepdims=True)
        acc = alpha * acc + _dot(vt, p.astype(BF16))
        return m_new, l, acc

    def off_body(kb, carry):
        k0 = pl.multiple_of(kb * tq, tq)
        cst = slope * (k0 - q0).astype(F32)
        return update(scores(k0) + rowb_scr[...], cst, *carry, vt_scr[:, pl.ds(k0, tq)])

    m0 = jnp.full((1, 2 * tq), NEG, F32)
    l0 = jnp.zeros((1, 2 * tq), F32)
    a0 = jnp.zeros((LANE, 2 * tq), F32)
    carry = lax.fori_loop(0, qi, off_body, (m0, l0, a0))

    allowed = (r // CHUNK) <= (c // CHUNK)
    bias = jnp.where(allowed, slope * (c - jnp.abs(c - r)).astype(F32), NEG)
    k0 = pl.multiple_of(q0, tq)
    sd = jnp.maximum(scores(k0) + jnp.concatenate([bias, bias], axis=1), NEG)
    m, l, acc = update(sd, 0.0, *carry, vt_scr[:, pl.ds(k0, tq)])

    lp = lamp_ref[...]
    lam = (jnp.exp(jnp.sum(lp[0:1] * lp[1:2], axis=-1, keepdims=True))
           - jnp.exp(jnp.sum(lp[2:3] * lp[3:4], axis=-1, keepdims=True)) + lam_init)
    o = acc[:, :tq] / l[:, :tq] - lam * (acc[:, tq:] / l[:, tq:])
    y = o * lax.rsqrt(jnp.mean(o * o, axis=0, keepdims=True) + EPS)
    o_ref[...] = ((y.T * subg_ref[...]) * (1.0 - lam_init)).astype(BF16)


def _da_attention(proj, slopes, lamp, subg, batch, seq, lam_init):
    n = proj.shape[0]
    tq = min(512, seq)
    nq = seq // tq
    kern = functools.partial(_da_kernel, tq=tq, lam_init=lam_init)
    qspec = lambda off: pl.BlockSpec((tq, LANE), lambda b, h, i: (b * nq + i, off + h // 2))
    kspec = lambda off: pl.BlockSpec((seq, LANE), lambda b, h, i: (b, off + h // 2))
    return pl.pallas_call(
        kern,
        grid=(batch, DA_HEADS, nq),
        in_specs=[pl.BlockSpec(memory_space=pltpu.SMEM),
                  qspec(0), qspec(4), kspec(8), kspec(12),
                  pl.BlockSpec((seq, LANE), lambda b, h, i: (b, 16 + h)),
                  pl.BlockSpec((4, DA_HEAD_DIM), lambda b, h, i: (0, 0)),
                  pl.BlockSpec((1, LANE), lambda b, h, i: (0, 0))],
        out_specs=pl.BlockSpec((tq, LANE), lambda b, h, i: (b * nq + i, h)),
        out_shape=jax.ShapeDtypeStruct((n, DA_HEADS * 2 * DA_HEAD_DIM), BF16),
        scratch_shapes=[pltpu.VMEM((LANE, seq), BF16), pltpu.VMEM((tq, 2 * tq), F32)],
        compiler_params=_params(("parallel", "parallel", "arbitrary")),
        name="da_attn",
    )(slopes, proj, proj, proj, proj, proj, lamp, subg)


def _sb_kernel(q_ref, k_ref, v_ref, o_ref, acc_scr, rsum_scr, *, tqb, ts):
    qi = pl.program_id(2)
    nsub = tqb // ts
    lane = lax.broadcasted_iota(jnp.int32, (1, LANE), 1)
    r = lax.broadcasted_iota(jnp.int32, (ts, ts), 0)
    c = lax.broadcasted_iota(jnp.int32, (ts, ts), 1)
    upper = (r > c).astype(BF16)
    strict = c < r
    zero = jnp.zeros((), BF16)
    head_mask = [(lane // SB_HEAD_DIM) == hs for hs in range(2)]

    acc_scr[...] = jnp.zeros_like(acc_scr)
    rsum_scr[...] = jnp.zeros_like(rsum_scr)
    strict2 = jnp.concatenate([strict, strict], axis=0)

    def cond(carry):
        return carry[1] == 0

    def body(carry):
        j, _ = carry
        zs, vs, masks, kbs = [], [], [], []
        for sub in range(nsub):
            kb = qi * nsub + sub - j
            k0 = pl.multiple_of(jnp.maximum(kb, 0) * ts, ts)
            q_sub = q_ref[sub * ts:(sub + 1) * ts, :]
            q2 = jnp.concatenate([jnp.where(head_mask[0], q_sub, zero),
                                  jnp.where(head_mask[1], q_sub, zero)], axis=0)
            zs.append(_dot_nt(q2, k_ref[pl.ds(k0, ts), :]))
            vs.append(v_ref[pl.ds(k0, ts), :])
            valid = jnp.logical_and(jnp.logical_or(strict2, j > 0), kb >= 0)
            masks.append(jnp.where(valid, 1.0, 0.0))
            kbs.append(kb)
        z = jnp.concatenate(zs, axis=0)
        valid_f = jnp.concatenate(masks, axis=0)
        t = jnp.log1p(jnp.exp(-jnp.abs(z)))
        sp = valid_f * (jnp.maximum(z, 0.0) + t)
        logsig = jnp.minimum(z, 0.0) - t
        hi = sp.astype(BF16)
        lo = (sp - hi.astype(F32)).astype(BF16)
        later = _dot(hi, upper) + _dot(lo, upper)
        rsum = rsum_scr[...]
        a = (valid_f * jnp.exp(logsig - (rsum + later))).astype(BF16)
        acc_scr[...] += jnp.concatenate(
            [_dot(a[2 * ts * sub:2 * ts * (sub + 1)], vs[sub]) for sub in range(nsub)], axis=0)
        rsum = rsum + jnp.sum(sp, axis=-1, keepdims=True)
        rsum_scr[...] = rsum
        alive = jnp.full((2 * ts, 1), 2.0 * F32_EXP_ZERO, F32)
        for sub in range(nsub):
            alive = jnp.minimum(alive, jnp.where(kbs[sub] > 0, rsum[2 * ts * sub:2 * ts * (sub + 1)],
                                                 2.0 * F32_EXP_ZERO))
        done = (jnp.min(alive) > F32_EXP_ZERO).astype(jnp.int32)
        return j + 1, done

    lax.while_loop(cond, body, (jnp.int32(0), jnp.int32(0)))
    for sub in range(nsub):
        base = 2 * ts * sub
        o_ref[sub * ts:(sub + 1) * ts, :] = jnp.where(
            head_mask[0], acc_scr[base:base + ts, :], acc_scr[base + ts:base + 2 * ts, :]).astype(BF16)


def _sb_attention(proj, batch, seq):
    n = proj.shape[0]
    tqb = min(512, seq)
    ts = 128
    nq = seq // tqb
    pairs = SB_HEADS // 2
    kern = functools.partial(_sb_kernel, tqb=tqb, ts=ts)
    return pl.pallas_call(
        kern,
        grid=(batch, pairs, nq),
        in_specs=[pl.BlockSpec((tqb, LANE), lambda b, p, i: (b * nq + i, 24 + p)),
                  pl.BlockSpec((seq, LANE), lambda b, p, i: (b, 32 + p)),
                  pl.BlockSpec((seq, LANE), lambda b, p, i: (b, 40 + p))],
        out_specs=pl.BlockSpec((tqb, LANE), lambda b, p, i: (b * nq + i, p)),
        out_shape=jax.ShapeDtypeStruct((n, SB_HEADS * SB_HEAD_DIM), BF16),
        scratch_shapes=[pltpu.VMEM((2 * tqb, LANE), F32), pltpu.VMEM((2 * tqb, 1), F32)],
        compiler_params=_params(("parallel", "parallel", "arbitrary")),
        name="sb_attn",
    )(proj, proj, proj)


def _post_kernel(oa_ref, ob_ref, ga_ref, gb_ref, x_ref, wpa_ref, wpb_ref, wo_ref, mod_ref, g2_ref,
                 rw_ref, rb_ref, x_out, h_out, idx_out, gate_out, rank_out, cnt_out, cnt_scr, *, tm):
    @pl.when(pl.program_id(0) == 0)
    def _():
        cnt_scr[...] = jnp.zeros_like(cnt_scr)

    pa = _dot(oa_ref[...], wpa_ref[...])
    pb = _dot(ob_ref[...], wpb_ref[...])
    merged = (jax.nn.sigmoid(ga_ref[...].astype(F32)) * pa
              + jax.nn.sigmoid(gb_ref[...].astype(F32)) * pb)
    att = _dot(merged.astype(BF16), wo_ref[...])
    x = x_ref[...] + mod_ref[0, 2:3, :] * att
    x_out[...] = x
    h = _rms_mod(x, g2_ref[...], mod_ref[0, 4:5, :], mod_ref[0, 3:4, :])
    h_out[...] = h

    logits = jnp.dot(h, rw_ref[...], preferred_element_type=F32,
                     precision=lax.Precision.HIGHEST) + rb_ref[...]
    lane = lax.broadcasted_iota(jnp.int32, (1, LANE), 1)
    lane_f = lane.astype(F32)
    work = logits
    idx_acc = jnp.zeros((tm, LANE), F32)
    val_acc = jnp.zeros((tm, LANE), F32)
    onehot = jnp.zeros((tm, LANE), F32)
    sels = []
    top0 = None
    for k in range(TOP_K):
        mx = jnp.max(work, axis=-1, keepdims=True)
        sel = jnp.min(jnp.where(work == mx, lane_f, float(LANE)), axis=-1, keepdims=True)
        hit = lane_f == sel
        if k == 0:
            top0 = mx
        idx_acc = jnp.where(lane == k, sel, idx_acc)
        val_acc = jnp.where(lane == k, jnp.exp(mx - top0), val_acc)
        onehot = onehot + hit.astype(F32)
        work = jnp.where(hit, NEG * 2.0, work)
        sels.append(hit)
    gate_out[...] = val_acc / jnp.sum(val_acc, axis=-1, keepdims=True)
    idx_out[...] = idx_acc.astype(jnp.int32)

    rr = lax.broadcasted_iota(jnp.int32, (tm, tm), 0)
    cc = lax.broadcasted_iota(jnp.int32, (tm, tm), 1)
    lower = (cc < rr).astype(BF16)
    before = _dot(lower, onehot.astype(BF16)) + cnt_scr[...]
    rank_acc = jnp.zeros((tm, LANE), F32)
    for k in range(TOP_K):
        rk = jnp.sum(jnp.where(sels[k], before, 0.0), axis=-1, keepdims=True)
        rank_acc = jnp.where(lane == k, rk, rank_acc)
    rank_out[...] = rank_acc.astype(jnp.int32)
    cnt_scr[...] = cnt_scr[...] + jnp.sum(onehot, axis=0, keepdims=True)
    cnt_out[...] = cnt_scr[...]


def _post_attention(oa, ob, proj, x2, wpa, wpb, wo, modv, g2, rw, rb, seq):
    n, d = x2.shape
    tm = min(512, seq)
    tiles_per_batch = seq // tm
    row = lambda i: (i, 0)
    const = lambda i: (0, 0)
    kern = functools.partial(_post_kernel, tm=tm)
    return pl.pallas_call(
        kern,
        grid=(n // tm,),
        in_specs=[pl.BlockSpec((tm, d), row), pl.BlockSpec((tm, d), row),
                  pl.BlockSpec((tm, d), lambda i: (i, 6)), pl.BlockSpec((tm, d), lambda i: (i, 7)),
                  pl.BlockSpec((tm, d), row),
                  pl.BlockSpec((d, d), const), pl.BlockSpec((d, d), const), pl.BlockSpec((d, d), const),
                  pl.BlockSpec((1, 8, d), lambda i: (i // tiles_per_batch, 0, 0)),
                  pl.BlockSpec((1, d), const),
                  pl.BlockSpec((d, LANE), const), pl.BlockSpec((1, LANE), const)],
        out_specs=[pl.BlockSpec((tm, d), row), pl.BlockSpec((tm, d), row),
                   pl.BlockSpec((tm, LANE), row), pl.BlockSpec((tm, LANE), row),
                   pl.BlockSpec((tm, LANE), row), pl.BlockSpec((1, LANE), const)],
        out_shape=[jax.ShapeDtypeStruct((n, d), F32), jax.ShapeDtypeStruct((n, d), F32),
                   jax.ShapeDtypeStruct((n, LANE), jnp.int32), jax.ShapeDtypeStruct((n, LANE), F32),
                   jax.ShapeDtypeStruct((n, LANE), jnp.int32), jax.ShapeDtypeStruct((1, LANE), F32)],
        scratch_shapes=[pltpu.VMEM((1, LANE), F32)],
        compiler_params=_params(("arbitrary",)),
        name="post_attn",
    )(oa, ob, proj, proj, x2, wpa, wpb, wo, modv, g2, rw, rb)


ROWS_PER_ISSUE = 8


def _row_copy(src_ref, src_row, dst_ref, dst_row, sem):
    return pltpu.make_async_copy(src_ref.at[pl.ds(src_row, 1)], dst_ref.at[pl.ds(dst_row, 1)], sem)


def _scatter_kernel(dest_ref, zoff_ref, zcnt_ref, h_ref, xs_ref, zbuf, sem, zsem, *, tm):
    i = pl.program_id(0)
    base = i * (tm * TOP_K)

    def row_copies(g, fn):
        for rr in range(ROWS_PER_ISSUE):
            row = g * ROWS_PER_ISSUE + rr
            for k in range(TOP_K):
                fn(_row_copy(h_ref, row, xs_ref, dest_ref[base + row * TOP_K + k], sem))

    def issue(g, _):
        row_copies(g, lambda cp: cp.start())
        return 0

    lax.fori_loop(0, tm // ROWS_PER_ISSUE, issue, 0)

    @pl.when(i == 0)
    def _():
        zbuf[...] = jnp.zeros_like(zbuf)

        def pad_copies(e, fn):
            cnt = zcnt_ref[e]
            off = zoff_ref[e]
            n_single = jnp.minimum((SUBLANE - (off & (SUBLANE - 1))) & (SUBLANE - 1), cnt)
            for s in range(SUBLANE - 1):
                @pl.when(s < n_single)
                def _():
                    fn(pltpu.make_async_copy(zbuf.at[pl.ds(0, 1)], xs_ref.at[pl.ds(off + s, 1)], zsem))

            off8 = off + n_single
            rem = cnt - n_single
            for bit in range(PAD_BITS - 1, SUBLANE.bit_length() - 2, -1):
                size = 1 << bit

                @pl.when(((rem >> bit) & 1) == 1)
                def _():
                    start = pl.multiple_of(off8 + ((rem >> (bit + 1)) << (bit + 1)), SUBLANE)
                    fn(pltpu.make_async_copy(zbuf.at[pl.ds(0, size)], xs_ref.at[pl.ds(start, size)], zsem))

        def pad_issue(e, _):
            pad_copies(e, lambda cp: cp.start())
            return 0

        def pad_drain(e, _):
            pad_copies(e, lambda cp: cp.wait())
            return 0

        lax.fori_loop(0, N_EXPERTS, pad_issue, 0)
        lax.fori_loop(0, N_EXPERTS, pad_drain, 0)

        zrows = zbuf.shape[0]

        def tail_copy(t):
            start = pl.multiple_of(t * zrows, zrows)
            return pltpu.make_async_copy(zbuf, xs_ref.at[pl.ds(start, zrows)], zsem)

        def tail_issue(t, _):
            tail_copy(t).start()
            return 0

        def tail_drain(t, _):
            tail_copy(t).wait()
            return 0

        first_tail = zoff_ref[N_EXPERTS] // zrows
        lax.fori_loop(first_tail, xs_ref.shape[0] // zrows, tail_issue, 0)
        lax.fori_loop(first_tail, xs_ref.shape[0] // zrows, tail_drain, 0)

    def drain(g, _):
        row_copies(g, lambda cp: cp.wait())
        return 0

    lax.fori_loop(0, tm // ROWS_PER_ISSUE, drain, 0)


def _scatter_rows(dest, zoff, zcnt, h2, n_slots):
    n, d = h2.shape
    tm = 256
    kern = functools.partial(_scatter_kernel, tm=tm)
    return pl.pallas_call(
        kern,
        grid_spec=pltpu.PrefetchScalarGridSpec(
            num_scalar_prefetch=3,
            grid=(n // tm,),
            in_specs=[pl.BlockSpec((tm, d), lambda i, *_: (i, 0))],
            out_specs=pl.BlockSpec(memory_space=pl.ANY),
            scratch_shapes=[pltpu.VMEM((1 << (PAD_BITS - 1), d), F32),
                            pltpu.SemaphoreType.DMA, pltpu.SemaphoreType.DMA]),
        out_shape=jax.ShapeDtypeStruct((n_slots, d), F32),
        compiler_params=_params(("arbitrary",)),
        name="moe_scatter",
    )(dest, zoff, zcnt, h2)


def _expert_kernel(bexp_ref, nused_ref, xs_ref, wgu_ref, bgu_ref, wd_ref, bd_ref, y_ref,
                   wgu_bf, wd_bf, *, f):
    b = pl.program_id(0)
    active = b < nused_ref[0]

    @pl.when(active)
    def _():
        prev = bexp_ref[jnp.maximum(b - 1, 0)]

        @pl.when(jnp.logical_or(b == 0, bexp_ref[b] != prev))
        def _():
            wgu_bf[...] = wgu_ref[0].astype(BF16)
            wd_bf[...] = wd_ref[0].astype(BF16)

        gu = _dot(xs_ref[...].astype(BF16), wgu_bf[...]) + bgu_ref[0]
        g = jnp.minimum(gu[:, :f], SWIGLU_LIMIT)
        u = jnp.clip(gu[:, f:], -SWIGLU_LIMIT, SWIGLU_LIMIT)
        act = (u + 1.0) * (g * jax.nn.sigmoid(SWIGLU_ALPHA * g))
        y_ref[...] = _dot(act.astype(BF16), wd_bf[...]) + bd_ref[0]

    @pl.when(jnp.logical_not(active))
    def _():
        y_ref[...] = jnp.zeros_like(y_ref)


def _experts(blk_exp, n_used, xs, w_gu, b_gu, w_down, b_down, layer):
    n_slots, d = xs.shape
    depth, e, _, f2 = w_gu.shape
    f = f2 // 2
    n_blocks = n_slots // EXPERT_BLOCK
    kern = functools.partial(_expert_kernel, f=f)

    def blk(b, bexp, nused):
        return jnp.minimum(b, nused[0] - 1)

    def wsel(b, bexp, nused):
        return (layer * e + bexp[blk(b, bexp, nused)], 0, 0)

    return pl.pallas_call(
        kern,
        grid_spec=pltpu.PrefetchScalarGridSpec(
            num_scalar_prefetch=2,
            grid=(n_blocks,),
            in_specs=[pl.BlockSpec((EXPERT_BLOCK, d), lambda b, bexp, nused: (blk(b, bexp, nused), 0)),
                      pl.BlockSpec((1, d, f2), wsel), pl.BlockSpec((1, 1, f2), wsel),
                      pl.BlockSpec((1, f, d), wsel), pl.BlockSpec((1, 1, d), wsel)],
            out_specs=pl.BlockSpec((EXPERT_BLOCK, d), lambda b, bexp, nused: (b, 0)),
            scratch_shapes=[pltpu.VMEM((d, f2), BF16), pltpu.VMEM((f, d), BF16)]),
        out_shape=jax.ShapeDtypeStruct((n_slots, d), F32),
        compiler_params=_params(("arbitrary",)),
        name="moe_experts",
    )(blk_exp, n_used, xs, w_gu.reshape(depth * e, d, f2), b_gu.reshape(depth * e, 1, f2),
      w_down.reshape(depth * e, f, d), b_down.reshape(depth * e, 1, d))


def _combine_kernel(dest_ref, y_ref, gate_ref, x_ref, mod_ref, fg_ref, o_ref, buf, sem, *, tm, final):
    base = pl.program_id(0) * (tm * TOP_K)

    def row_copies(g, fn):
        for rr in range(ROWS_PER_ISSUE):
            row = g * ROWS_PER_ISSUE + rr
            for k in range(TOP_K):
                fn(_row_copy(y_ref, dest_ref[base + row * TOP_K + k], buf.at[k], row, sem))

    def issue(g, _):
        row_copies(g, lambda cp: cp.start())
        return 0

    lax.fori_loop(0, tm // ROWS_PER_ISSUE, issue, 0)

    def drain(g, _):
        row_copies(g, lambda cp: cp.wait())
        return 0

    lax.fori_loop(0, tm // ROWS_PER_ISSUE, drain, 0)

    gate = gate_ref[...]
    moe = gate[:, 0:1] * buf[0]
    for k in range(1, TOP_K):
        moe = moe + gate[:, k:k + 1] * buf[k]
    x = x_ref[...] + mod_ref[0, 5:6, :] * moe
    if final:
        x = (x * lax.rsqrt(jnp.mean(x * x, axis=-1, keepdims=True) + EPS)) * fg_ref[...]
    o_ref[...] = x


def _combine(dest, y, gates, x2, modv, final_g, seq, final):
    n, d = x2.shape
    tm = 256
    tiles_per_batch = seq // tm
    kern = functools.partial(_combine_kernel, tm=tm, final=final)
    return pl.pallas_call(
        kern,
        grid_spec=pltpu.PrefetchScalarGridSpec(
            num_scalar_prefetch=1,
            grid=(n // tm,),
            in_specs=[pl.BlockSpec(memory_space=pl.ANY),
                      pl.BlockSpec((tm, LANE), lambda i, dest: (i, 0)),
                      pl.BlockSpec((tm, d), lambda i, dest: (i, 0)),
                      pl.BlockSpec((1, 8, d), lambda i, dest: (i // tiles_per_batch, 0, 0)),
                      pl.BlockSpec((1, d), lambda i, dest: (0, 0))],
            out_specs=pl.BlockSpec((tm, d), lambda i, dest: (i, 0)),
            scratch_shapes=[pltpu.VMEM((TOP_K, tm, d), F32), pltpu.SemaphoreType.DMA]),
        out_shape=jax.ShapeDtypeStruct((n, d), F32),
        compiler_params=_params(("arbitrary",)),
        name="moe_combine",
    )(dest, y, gates, x2, modv, final_g)


def _routing_tables(counts_f, eidx, rank, n_blocks):
    counts = counts_f[0, :N_EXPERTS].astype(jnp.int32)
    padded = ((counts + EXPERT_BLOCK - 1) // EXPERT_BLOCK) * EXPERT_BLOCK
    pad_end = jnp.cumsum(padded)
    pad_start = pad_end - padded
    dest = (pad_start[eidx] + rank).reshape(-1).astype(jnp.int32)
    blk_first = jnp.arange(n_blocks, dtype=jnp.int32) * EXPERT_BLOCK
    blk_exp = jnp.minimum(jnp.sum(pad_end[None, :] <= blk_first[:, None], axis=1),
                          N_EXPERTS - 1).astype(jnp.int32)
    n_used = (pad_end[-1:] // EXPERT_BLOCK).astype(jnp.int32)
    pad_first = jnp.concatenate([pad_start + counts, pad_end[-1:]]).astype(jnp.int32)
    pad_count = (padded - counts).astype(jnp.int32)
    return dest, blk_exp, n_used, pad_first, pad_count


def kernel(x, c, w_ada, b_ada, norm1_g, w_in, lam_q1, lam_k1, lam_q2, lam_k2, subln_g, w_pa, w_pb,
           w_o, norm2_g, router_w, router_b, w_gu, b_gu, w_down, b_down, final_g):
    batch, seq, d = x.shape
    depth = w_ada.shape[0]
    n = batch * seq
    n_blocks = -(-(n * TOP_K) // EXPERT_BLOCK) + N_EXPERTS
    n_slots = n_blocks * EXPERT_BLOCK

    c8 = jnp.zeros((8, d), F32).at[:batch].set(c)
    mod = _ada_mod(c8, w_ada, b_ada)
    slopes = jnp.asarray([2.0 ** (-8.0 * (i + 1) / DA_HEADS) for i in range(DA_HEADS)], F32)
    fg = final_g.reshape(1, d)

    x2 = x.reshape(n, d)
    for l in range(depth):
        modv = jnp.zeros((batch, 8, d), F32).at[:, :6].set(mod[l, :batch].reshape(batch, 6, d))
        lam_init = 0.8 - 0.6 * math.exp(-0.3 * l)
        proj = _inproj(x2, modv, norm1_g[l].reshape(1, d), w_in[l].astype(BF16), seq)
        lamp = jnp.stack([lam_q1[l], lam_k1[l], lam_q2[l], lam_k2[l]])
        oa = _da_attention(proj, slopes, lamp, subln_g[l].reshape(1, LANE), batch, seq, lam_init)
        ob = _sb_attention(proj, batch, seq)
        rw = jnp.zeros((d, LANE), F32).at[:, :N_EXPERTS].set(router_w[l])
        rb = jnp.full((1, LANE), NEG, F32).at[0, :N_EXPERTS].set(router_b[l])
        x2, h2, eidx, gates, rank, counts = _post_attention(
            oa, ob, proj, x2, w_pa[l].astype(BF16), w_pb[l].astype(BF16), w_o[l].astype(BF16),
            modv, norm2_g[l].reshape(1, d), rw, rb, seq)
        dest, blk_exp, n_used, pad_first, pad_count = _routing_tables(
            counts, eidx[:, :TOP_K], rank[:, :TOP_K], n_blocks)
        xs = _scatter_rows(dest, pad_first, pad_count, h2, n_slots)
        y = _experts(blk_exp, n_used, xs, w_gu, b_gu, w_down, b_down, l)
        x2 = _combine(dest, y, gates, x2, modv, fg, seq, final=(l == depth - 1))
    return x2.reshape(batch, seq, d)
```
